```python
import jax, jax.numpy as jnp
from jax import lax
import numpy as np

D_MODEL = 1024
BATCH = 16
SEQ = 256
DEPTH = 2
DEC_BATCH = 8
DEC_SEQ = 2048
PAST_LEN = 256

GRID_W = 64
N_MIXERS = 2
N_CHUNK_LAYERS = (DEPTH + 1) // 2
N_MLA_LAYERS = DEPTH // 2
CHUNK = 128
D_INNER = 2 * D_MODEL
N_GROUPS = 8
GROUP_DIM = D_INNER // N_GROUPS
N_HEADS = 8
NOPE_DIM = 128
ROPE_DIM = 64
V_DIM = 128
Q_RANK = 384
KV_RANK = 256
Q_BLOCK = 128
ROPE_BASE = 10000.0
D_FF = 2816
CONV_W = 3
EPS = 1e-6

kernel_name = "hybrid_chunkmlp_mla_convffn_diffusion_step"


def rms_norm(x, g):
    xf = x.astype(jnp.float32)
    y = xf * lax.rsqrt(jnp.mean(xf * xf, axis=-1, keepdims=True) + EPS)
    return (y * g.astype(jnp.float32)).astype(x.dtype)


def ada_params(cond, w, b):
    m = (jax.nn.silu(cond) @ w + b)[..., None, :]
    return jnp.split(m, 6, axis=-1)


def modulate(h, shift, scale):
    return h * (1 + scale) + shift


def grid_positions(T):
    rows = T // GRID_W
    r = jnp.repeat(jnp.arange(rows, dtype=jnp.float32), GRID_W)
    col = jnp.tile(jnp.arange(GRID_W, dtype=jnp.float32), rows)
    return r, col


def rope_axial(x, r, col):
    axis_dim = ROPE_DIM // 2
    half = axis_dim // 2
    inv = ROPE_BASE ** (-jnp.arange(half, dtype=jnp.float32) / half)
    xf = x.astype(jnp.float32)
    outs = []
    for a, pos in enumerate((r, col)):
        ang = pos[:, None] * inv[None, :]
        cos = jnp.cos(ang)[:, None, :]
        sin = jnp.sin(ang)[:, None, :]
        xa = xf[..., a * axis_dim:(a + 1) * axis_dim]
        x1, x2 = xa[..., :half], xa[..., half:]
        outs.append(x1 * cos - x2 * sin)
        outs.append(x1 * sin + x2 * cos)
    return jnp.concatenate(outs, axis=-1).astype(x.dtype)


def chunk_mlp_mixer(x, w_in, b_in, g_v, w_s, b_s, w_out):
    B, T, _ = x.shape
    h = jax.nn.gelu(x @ w_in + b_in, approximate=False)
    u, v = jnp.split(h, 2, axis=-1)
    v = rms_norm(v, g_v).reshape(B, T // CHUNK, CHUNK, N_GROUPS, GROUP_DIM)
    mixed = jnp.einsum('gpq,bnqgd->bnpgd', w_s, v) + b_s.T[None, None, :, :, None]
    return (u * mixed.reshape(B, T, D_INNER)) @ w_out


def conv_ffn(x, w_up, conv_w, conv_b, w_down):
    h = x @ w_up
    hp = jnp.pad(h, ((0, 0), (1, 1), (0, 0)))
    h = hp[:, :-2] * conv_w[0] + hp[:, 1:-1] * conv_w[1] + hp[:, 2:] * conv_w[2] + conv_b
    g, val = jnp.split(h, 2, axis=-1)
    return (jax.nn.silu(g) * val) @ w_down


def mla_project_kv(x, w_dkv, g_kv):
    kv = x @ w_dkv
    return rms_norm(kv[..., :KV_RANK], g_kv), kv[..., KV_RANK:]


def mla_queries(x, w_dq, g_q, w_uq):
    B, T, _ = x.shape
    q = (rms_norm(x @ w_dq, g_q) @ w_uq).reshape(B, T, N_HEADS, NOPE_DIM + ROPE_DIM)
    return q[..., :NOPE_DIM], q[..., NOPE_DIM:]


def mla_expand(ckv, w_uk, w_uv):
    B, S, _ = ckv.shape
    k = (ckv @ w_uk).reshape(B, S, N_HEADS, NOPE_DIM)
    v = (ckv @ w_uv).reshape(B, S, N_HEADS, V_DIM)
    return k, v


def mla_attend(q_nope, q_rope, k_nope, k_rope, v):
    scale = (NOPE_DIM + ROPE_DIM) ** -0.5
    s = (jnp.einsum('bqhd,bshd->bhqs', q_nope, k_nope)
         + jnp.einsum('bqhr,bsr->bhqs', q_rope, k_rope))
    p = jax.nn.softmax(s.astype(jnp.float32) * scale, axis=-1).astype(v.dtype)
    return jnp.einsum('bhqs,bshv->bqhv', p, v)


def mla_context(x, w_dq, g_q, w_uq, w_dkv, g_kv, w_uk, w_uv, w_o):
    B, T, _ = x.shape
    ckv, k_rope = mla_project_kv(x, w_dkv, g_kv)
    q_nope, q_rope = mla_queries(x, w_dq, g_q, w_uq)
    k, v = mla_expand(ckv, w_uk, w_uv)
    o = mla_attend(q_nope, q_rope, k, k_rope, v)
    return o.reshape(B, T, N_HEADS * V_DIM) @ w_o, ckv, k_rope


def mla_latent(x, ckv_ctx, krope_ctx, w_dq, g_q, w_uq, w_dkv, g_kv, w_uk, w_uv, w_o):
    B, T, _ = x.shape
    r, col = grid_positions(T)
    ckv, k_rope = mla_project_kv(x, w_dkv, g_kv)
    k_rope = rope_axial(k_rope[:, :, None, :], r, col)[:, :, 0, :]
    q_nope, q_rope = mla_queries(x, w_dq, g_q, w_uq)
    q_rope = rope_axial(q_rope, r, col)
    k_all, v_all = mla_expand(jnp.concatenate([ckv, ckv_ctx.astype(ckv.dtype)], axis=1), w_uk, w_uv)
    kr_all = jnp.concatenate([k_rope, krope_ctx.astype(k_rope.dtype)], axis=1)
    nb = T // Q_BLOCK
    qn_b = q_nope.reshape(B, nb, Q_BLOCK, N_HEADS, NOPE_DIM).transpose(1, 0, 2, 3, 4)
    qr_b = q_rope.reshape(B, nb, Q_BLOCK, N_HEADS, ROPE_DIM).transpose(1, 0, 2, 3, 4)
    o = lax.map(lambda qs: mla_attend(qs[0], qs[1], k_all, kr_all, v_all), (qn_b, qr_b))
    o = o.transpose(1, 0, 2, 3, 4).reshape(B, T, N_HEADS * V_DIM)
    return o @ w_o


def trunk(x, cond, p, ctx_ckv=None, ctx_krope=None):
    is_context = ctx_ckv is None
    new_ckv, new_krope = [], []
    for i in range(DEPTH):
        sh1, sc1, g1, sh2, sc2, g2 = ada_params(cond, p['ada_w'][i], p['ada_b'][i])
        h = modulate(rms_norm(x, p['norm_g'][i, 0]), sh1, sc1)
        j = i // N_MIXERS
        if i % N_MIXERS == 0:
            y = chunk_mlp_mixer(h, p['gm_w_in'][j], p['gm_b_in'][j], p['gm_g_v'][j],
                                p['gm_w_s'][j], p['gm_b_s'][j], p['gm_w_out'][j])
        else:
            mla_w = (p['mla_w_dq'][j], p['mla_g_q'][j], p['mla_w_uq'][j], p['mla_w_dkv'][j],
                     p['mla_g_kv'][j], p['mla_w_uk'][j], p['mla_w_uv'][j], p['mla_w_o'][j])
            if is_context:
                y, ckv, kr = mla_context(h, *mla_w)
                new_ckv.append(ckv)
                new_krope.append(kr)
            else:
                y = mla_latent(h, ctx_ckv[:, j], ctx_krope[:, j], *mla_w)
        x = x + g1 * rms_norm(y, p['norm_g'][i, 1])
        h = modulate(rms_norm(x, p['norm_g'][i, 2]), sh2, sc2)
        y = conv_ffn(h, p['ffn_w_up'][i], p['ffn_conv_w'][i], p['ffn_conv_b'][i], p['ffn_w_down'][i])
        x = x + g2 * rms_norm(y, p['norm_g'][i, 3])
    return x, new_ckv, new_krope


def setup_inputs(seed: int = 0) -> dict:
    key = jax.random.key(seed)
    ks = iter(jax.random.split(key, 40))
    f32 = jnp.float32

    def nrm(shape, scale=1.0):
        return jax.random.normal(next(ks), shape, f32) * scale

    D = D_MODEL
    return {
        'x_prompt': nrm((BATCH, SEQ, D)),
        'x_sample': nrm((DEC_BATCH, DEC_SEQ, D)),
        'cache_ckv': nrm((DEC_BATCH, N_MLA_LAYERS, PAST_LEN, KV_RANK)),
        'cache_krope': nrm((DEC_BATCH, N_MLA_LAYERS, PAST_LEN, ROPE_DIM)),
        'c': nrm((DEC_BATCH, D)),
        'c_ctx': nrm((D,)),
        'ada_w': nrm((DEPTH, D, 6 * D), 0.5 * D ** -0.5),
        'ada_b': nrm((DEPTH, 6 * D), 0.02),
        'norm_g': 1.0 + nrm((DEPTH, 4, D), 0.02),
        'gm_w_in': nrm((N_CHUNK_LAYERS, D, 2 * D_INNER), D ** -0.5),
        'gm_b_in': nrm((N_CHUNK_LAYERS, 2 * D_INNER), 0.02),
        'gm_g_v': 1.0 + nrm((N_CHUNK_LAYERS, D_INNER), 0.02),
        'gm_w_s': nrm((N_CHUNK_LAYERS, N_GROUPS, CHUNK, CHUNK), CHUNK ** -0.5),
        'gm_b_s': 1.0 + nrm((N_CHUNK_LAYERS, N_GROUPS, CHUNK), 0.02),
        'gm_w_out': nrm((N_CHUNK_LAYERS, D_INNER, D), D_INNER ** -0.5),
        'mla_w_dq': nrm((N_MLA_LAYERS, D, Q_RANK), D ** -0.5),
        'mla_g_q': 1.0 + nrm((N_MLA_LAYERS, Q_RANK), 0.02),
        'mla_w_uq': nrm((N_MLA_LAYERS, Q_RANK, N_HEADS * (NOPE_DIM + ROPE_DIM)), Q_RANK ** -0.5),
        'mla_w_dkv': nrm((N_MLA_LAYERS, D, KV_RANK + ROPE_DIM), D ** -0.5),
        'mla_g_kv': 1.0 + nrm((N_MLA_LAYERS, KV_RANK), 0.02),
        'mla_w_uk': nrm((N_MLA_LAYERS, KV_RANK, N_HEADS * NOPE_DIM), KV_RANK ** -0.5),
        'mla_w_uv': nrm((N_MLA_LAYERS, KV_RANK, N_HEADS * V_DIM), KV_RANK ** -0.5),
        'mla_w_o': nrm((N_MLA_LAYERS, N_HEADS * V_DIM, D), (N_HEADS * V_DIM) ** -0.5),
        'ffn_w_up': nrm((DEPTH, D, 2 * D_FF), D ** -0.5),
        'ffn_conv_w': nrm((DEPTH, CONV_W, 2 * D_FF), CONV_W ** -0.5),
        'ffn_conv_b': nrm((DEPTH, 2 * D_FF), 0.02),
        'ffn_w_down': nrm((DEPTH, D_FF, D), D_FF ** -0.5),
    }


def reference(x_prompt, x_sample, cache_ckv, cache_krope, c, c_ctx,
              ada_w, ada_b, norm_g,
              gm_w_in, gm_b_in, gm_g_v, gm_w_s, gm_b_s, gm_w_out,
              mla_w_dq, mla_g_q, mla_w_uq, mla_w_dkv, mla_g_kv, mla_w_uk, mla_w_uv, mla_w_o,
              ffn_w_up, ffn_conv_w, ffn_conv_b, ffn_w_down):
    p = dict(ada_w=ada_w, ada_b=ada_b, norm_g=norm_g,
             gm_w_in=gm_w_in, gm_b_in=gm_b_in, gm_g_v=gm_g_v, gm_w_s=gm_w_s, gm_b_s=gm_b_s,
             gm_w_out=gm_w_out,
             mla_w_dq=mla_w_dq, mla_g_q=mla_g_q, mla_w_uq=mla_w_uq, mla_w_dkv=mla_w_dkv,
             mla_g_kv=mla_g_kv, mla_w_uk=mla_w_uk, mla_w_uv=mla_w_uv, mla_w_o=mla_w_o,
             ffn_w_up=ffn_w_up, ffn_conv_w=ffn_conv_w, ffn_conv_b=ffn_conv_b, ffn_w_down=ffn_w_down)
    y_prompt, ckv_list, krope_list = trunk(x_prompt, c_ctx, p)
    new_ckv = jnp.stack(ckv_list, axis=1)
    new_krope = jnp.stack(krope_list, axis=1)
    y_sample, _, _ = trunk(x_sample, c, p, cache_ckv, cache_krope)
    return (y_prompt, y_sample, new_ckv, new_krope)
```

```python
import functools

import jax
import jax.numpy as jnp
import numpy as np
from jax import lax
from jax.experimental import pallas as pl
from jax.experimental.pallas import tpu as pltpu

D_MODEL = 1024
GRID_W = 64
CHUNK = 128
D_INNER = 2 * D_MODEL
N_GROUPS = 8
GROUP_DIM = D_INNER // N_GROUPS
N_HEADS = 8
NOPE_DIM = 128
ROPE_DIM = 64
V_DIM = 128
Q_RANK = 384
KV_RANK = 256
ROPE_BASE = 10000.0
D_FF = 2816
EPS = 1e-6

LANES = 128
BF16_SUBLANES = 16
MXU_DIM = 256
VMEM_LIMIT_BYTES = 56 * 1024 * 1024

HEAD_PAD = 2 * LANES
ROPE_PAD = LANES
N_COND = 16
HALO = BF16_SUBLANES
FF_TILE = MXU_DIM

F32 = jnp.float32
BF16 = jnp.bfloat16


def _dot(a, b):
    return jnp.dot(a, b, preferred_element_type=F32)


def _rms(x, g):
    return x * lax.rsqrt(jnp.mean(x * x, axis=-1, keepdims=True) + EPS) * g


def _sigmoid(x):
    return 1.0 / (1.0 + jnp.exp(-x))


def _gelu(x):
    return 0.5 * x * (1.0 + lax.erf(x * np.float32(np.sqrt(0.5))))


def _params(*sem):
    return pltpu.CompilerParams(dimension_semantics=sem,
                                vmem_limit_bytes=VMEM_LIMIT_BYTES)


def _const_spec(shape):
    zeros = (0,) * len(shape)
    return pl.BlockSpec(shape, lambda *_: zeros, pipeline_mode=pl.Buffered(1))


def _ada_kernel(cond_ref, w_ref, b_ref, o_ref):
    c = cond_ref[...]
    s = (c * _sigmoid(c)).astype(BF16)
    o_ref[0] = _dot(s, w_ref[0].astype(BF16)) + b_ref[0]


def _ada(cond, ada_w, ada_b):
    depth, d, n = ada_w.shape
    tn = 512
    return pl.pallas_call(
        _ada_kernel,
        grid=(depth, n // tn),
        in_specs=[
            pl.BlockSpec((N_COND, d), lambda i, j: (0, 0)),
            pl.BlockSpec((1, d, tn), lambda i, j: (i, 0, j)),
            pl.BlockSpec((1, 1, tn), lambda i, j: (i, 0, j)),
        ],
        out_specs=pl.BlockSpec((1, N_COND, tn), lambda i, j: (i, 0, j)),
        out_shape=jax.ShapeDtypeStruct((depth, N_COND, n), F32),
        compiler_params=_params("parallel", "parallel"),
        name="ada",
    )(cond, ada_w, ada_b.reshape(depth, 1, n))


def _mixer_kernel(x_ref, mod_ref, ng_ref, w_in_ref, b_in_ref, gv_ref, ws_ref,
                  bs_ref, w_out_ref, o_ref, *, tm):
    x = x_ref[...]
    m = mod_ref[0]
    ng = ng_ref[...]
    hb = (_rms(x, ng[0:1]) * (1.0 + m[1:2]) + m[0:1]).astype(BF16)
    v = _gelu(_dot(hb, w_in_ref[:, D_INNER:]) + b_in_ref[:, D_INNER:])
    rs = lax.rsqrt(jnp.mean(v * v, axis=-1, keepdims=True) + EPS)
    acc = jnp.zeros((tm, D_MODEL), F32)
    for g in range(N_GROUPS):
        lo, hi = g * GROUP_DIM, (g + 1) * GROUP_DIM
        vg = (v[:, lo:hi] * rs * gv_ref[:, lo:hi]).astype(BF16)
        w_s = ws_ref[g]
        b_s = bs_ref[:, g:g + 1]
        mixed = jnp.concatenate(
            [_dot(w_s, vg[n * CHUNK:(n + 1) * CHUNK]) + b_s
             for n in range(tm // CHUNK)], axis=0)
        u = _gelu(_dot(hb, w_in_ref[:, lo:hi]) + b_in_ref[:, lo:hi])
        acc = acc + _dot((u * mixed).astype(BF16), w_out_ref[lo:hi, :])
    o_ref[...] = x + m[2:3] * _rms(acc, ng[1:2])


def _mixer(x, mod, ng, w_in, b_in, g_v, w_s, b_s_t, w_out, *, tm, mod_row):
    n, d = x.shape
    return pl.pallas_call(
        functools.partial(_mixer_kernel, tm=tm),
        grid=(n // tm,),
        in_specs=[
            pl.BlockSpec((tm, d), lambda i: (i, 0)),
            pl.BlockSpec((1, 6, d), lambda i: (mod_row(i * tm), 0, 0)),
            _const_spec(ng.shape),
            _const_spec(w_in.shape),
            _const_spec(b_in.shape),
            _const_spec(g_v.shape),
            _const_spec(w_s.shape),
            _const_spec(b_s_t.shape),
            _const_spec(w_out.shape),
        ],
        out_specs=pl.BlockSpec((tm, d), lambda i: (i, 0)),
        out_shape=jax.ShapeDtypeStruct((n, d), F32),
        compiler_params=_params("parallel"),
        name="mixer",
    )(x, mod, ng, w_in, b_in, g_v, w_s, b_s_t, w_out)


def _ffn_kernel(*refs, tm, seq_len, fuse_oproj):
    if fuse_oproj:
        (x_ref, xp_ref, xn_ref, o_ref, op_ref, on_ref, wo_ref, *refs) = refs
    else:
        (x_ref, xp_ref, xn_ref, *refs) = refs
    (mod_ref, ng_ref, wg_ref, wv_ref, cwg_ref, cwv_ref, cbg_ref, cbv_ref,
     wd_ref, out_ref, hn_scr, acc_scr, x1_scr) = refs
    i = pl.program_id(0)
    j = pl.program_id(1)
    m = mod_ref[0]
    ng = ng_ref[...]

    def x1_of(xr, orf):
        if not fuse_oproj:
            return xr[...]
        return xr[...] + m[2:3] * _rms(_dot(orf[...], wo_ref[...]), ng[1:2])

    def hn_of(x1):
        return (_rms(x1, ng[2:3]) * (1.0 + m[4:5]) + m[3:4]).astype(BF16)

    @pl.when(j == 0)
    def _():
        x1 = x1_of(x_ref, o_ref if fuse_oproj else None)
        x1_scr[...] = x1
        hn_scr[HALO:HALO + tm] = hn_of(x1)
        row0 = i * tm
        at_start = (row0 % seq_len) == 0
        at_end = ((row0 + tm) % seq_len) == 0
        hp = hn_of(x1_of(xp_ref, op_ref if fuse_oproj else None))
        hn_scr[0:HALO] = jnp.where(at_start, jnp.zeros_like(hp), hp)
        hx = hn_of(x1_of(xn_ref, on_ref if fuse_oproj else None))
        hn_scr[HALO + tm:] = jnp.where(at_end, jnp.zeros_like(hx), hx)
        acc_scr[...] = jnp.zeros_like(acc_scr)

    hn = hn_scr[...]

    def conv(h, cw_ref, cb_ref):
        cw = cw_ref[...]
        return (h[HALO - 1:HALO - 1 + tm] * cw[0:1] + h[HALO:HALO + tm] * cw[1:2]
                + h[HALO + 1:HALO + 1 + tm] * cw[2:3] + cb_ref[...])

    cg = conv(_dot(hn, wg_ref[...]), cwg_ref, cbg_ref)
    cv = conv(_dot(hn, wv_ref[...]), cwv_ref, cbv_ref)
    t = (cg * _sigmoid(cg) * cv).astype(BF16)
    acc_scr[...] += _dot(t, wd_ref[...])

    @pl.when(j == pl.num_programs(1) - 1)
    def _():
        out_ref[...] = x1_scr[...] + m[5:6] * _rms(acc_scr[...], ng[3:4])


def _ffn(x, mod, ng, w_up, conv_w, conv_b, w_down, *, tm, seq_len, mod_row,
         o=None, w_o=None):
    n, d = x.shape
    fuse = o is not None
    assert seq_len % tm == 0 and tm % HALO == 0
    nj = D_FF // FF_TILE
    hb = tm // HALO
    last = n // HALO - 1

    def main(i, j):
        return (i, 0)

    def prev(i, j):
        return (jnp.maximum(i * hb - 1, 0), 0)

    def nxt(i, j):
        return (jnp.minimum((i + 1) * hb, last), 0)

    args = [x, x, x]
    specs = [pl.BlockSpec((tm, d), main), pl.BlockSpec((HALO, d), prev),
             pl.BlockSpec((HALO, d), nxt)]
    if fuse:
        args += [o, o, o, w_o]
        specs += [pl.BlockSpec((tm, d), main), pl.BlockSpec((HALO, d), prev),
                  pl.BlockSpec((HALO, d), nxt), _const_spec(w_o.shape)]
    args += [mod, ng, w_up, w_up, conv_w, conv_w, conv_b, conv_b, w_down]
    specs += [
        pl.BlockSpec((1, 6, d), lambda i, j: (mod_row(i * tm), 0, 0)),
        _const_spec(ng.shape),
        pl.BlockSpec((d, FF_TILE), lambda i, j: (0, j)),
        pl.BlockSpec((d, FF_TILE), lambda i, j: (0, j + nj)),
        pl.BlockSpec((3, FF_TILE), lambda i, j: (0, j)),
        pl.BlockSpec((3, FF_TILE), lambda i, j: (0, j + nj)),
        pl.BlockSpec((1, FF_TILE), lambda i, j: (0, j)),
        pl.BlockSpec((1, FF_TILE), lambda i, j: (0, j + nj)),
        pl.BlockSpec((FF_TILE, d), lambda i, j: (j, 0)),
    ]
    return pl.pallas_call(
        functools.partial(_ffn_kernel, tm=tm, seq_len=seq_len, fuse_oproj=fuse),
        grid=(n // tm, nj),
        in_specs=specs,
        out_specs=pl.BlockSpec((tm, d), main),
        out_shape=jax.ShapeDtypeStruct((n, d), F32),
        scratch_shapes=[
            pltpu.VMEM((tm + 2 * HALO, d), BF16),
            pltpu.VMEM((tm, d), F32),
            pltpu.VMEM((tm, d), F32),
        ],
        compiler_params=_params("parallel", "arbitrary"),
        name="ffn_oproj" if fuse else "ffn",
    )(*args)


def _mla_proj_kernel(*refs, rope):
    (x_ref, mod_ref, ng_ref, wdkv_ref, gkv_ref, wdq_ref, gq_ref, wuq_ref,
     wuqs_ref, wuk_ref, wuv_ref, *refs) = refs
    if rope:
        cos_ref, sin_ref, q_ref, k_ref, v_ref = refs
        cos = cos_ref[...]
        sin = sin_ref[...]
    else:
        q_ref, k_ref, v_ref, ckv_ref, kr_ref = refs
    m = mod_ref[0]
    ng = ng_ref[...]
    hb = (_rms(x_ref[...], ng[0:1]) * (1.0 + m[1:2]) + m[0:1]).astype(BF16)
    kv = _dot(hb, wdkv_ref[...])
    ckv = _rms(kv[:, :KV_RANK], gkv_ref[...])
    kr = kv[:, KV_RANK:KV_RANK + ROPE_PAD]
    if rope:
        kr = kr * cos + kv[:, KV_RANK + ROPE_PAD:] * sin
    else:
        ckv_ref[...] = ckv
        kr_ref[...] = kv[:, KV_RANK:KV_RANK + ROPE_DIM]
    krb = kr.astype(BF16)
    ql = _rms(_dot(hb, wdq_ref[...]), gq_ref[...]).astype(BF16)
    q = _dot(ql, wuq_ref[...])
    if rope:
        qs = _dot(ql, wuqs_ref[...])
    ckvb = ckv.astype(BF16)
    kn = _dot(ckvb, wuk_ref[...])
    v_ref[...] = _dot(ckvb, wuv_ref[...]).astype(BF16)
    for h in range(N_HEADS):
        lo = h * HEAD_PAD
        mid = lo + NOPE_DIM
        q_ref[:, lo:mid] = q[:, lo:mid].astype(BF16)
        qr = q[:, mid:lo + HEAD_PAD]
        if rope:
            qr = qr * cos + qs[:, h * ROPE_PAD:(h + 1) * ROPE_PAD] * sin
        q_ref[:, mid:lo + HEAD_PAD] = qr.astype(BF16)
        k_ref[:, lo:mid] = kn[:, h * NOPE_DIM:(h + 1) * NOPE_DIM].astype(BF16)
        k_ref[:, mid:lo + HEAD_PAD] = krb


def _mla_proj(x, mod, ng, w, *, tm, mod_row, rope_tables=None, seq_len=None):
    n, d = x.shape
    rope = rope_tables is not None
    args = [x, mod, ng, w["dkv"], w["g_kv"], w["dq"], w["g_q"], w["uq"],
            w["uq_sw"], w["uk"], w["uv"]]
    specs = [pl.BlockSpec((tm, d), lambda i: (i, 0)),
             pl.BlockSpec((1, 6, d), lambda i: (mod_row(i * tm), 0, 0))]
    specs += [_const_spec(a.shape) for a in args[2:]]
    row = lambda i: (i, 0)
    out_shape = [jax.ShapeDtypeStruct((n, N_HEADS * HEAD_PAD), BF16),
                 jax.ShapeDtypeStruct((n, N_HEADS * HEAD_PAD), BF16),
                 jax.ShapeDtypeStruct((n, N_HEADS * V_DIM), BF16)]
    out_specs = [pl.BlockSpec((tm, N_HEADS * HEAD_PAD), row),
                 pl.BlockSpec((tm, N_HEADS * HEAD_PAD), row),
                 pl.BlockSpec((tm, N_HEADS * V_DIM), row)]
    if rope:
        tiles_per_seq = seq_len // tm
        args += list(rope_tables)
        specs += [pl.BlockSpec((tm, ROPE_PAD), lambda i: (i % tiles_per_seq, 0))] * 2
    else:
        out_shape += [jax.ShapeDtypeStruct((n, KV_RANK), F32),
                      jax.ShapeDtypeStruct((n, ROPE_DIM), F32)]
        out_specs += [pl.BlockSpec((tm, KV_RANK), row),
                      pl.BlockSpec((tm, ROPE_DIM), row)]
    return pl.pallas_call(
        functools.partial(_mla_proj_kernel, rope=rope),
        grid=(n // tm,),
        in_specs=specs,
        out_specs=out_specs,
        out_shape=out_shape,
        compiler_params=_params("parallel"),
        name="mla_proj_rope" if rope else "mla_proj",
    )(*args)


def _kv_expand_kernel(ckv_ref, kr_ref, wuk_ref, wuv_ref, k_ref, v_ref):
    ckvb = ckv_ref[...].astype(BF16)
    kn = _dot(ckvb, wuk_ref[...])
    v_ref[...] = _dot(ckvb, wuv_ref[...]).astype(BF16)
    krb = kr_ref[...].astype(BF16)
    for h in range(N_HEADS):
        lo = h * HEAD_PAD
        k_ref[:, lo:lo + NOPE_DIM] = kn[:, h * NOPE_DIM:(h + 1) * NOPE_DIM].astype(BF16)
        k_ref[:, lo + NOPE_DIM:lo + HEAD_PAD] = krb


def _kv_expand(ckv, kr_pad, w_uk, w_uv, *, tm):
    n = ckv.shape[0]
    row = lambda i: (i, 0)
    return pl.pallas_call(
        _kv_expand_kernel,
        grid=(n // tm,),
        in_specs=[pl.BlockSpec((tm, KV_RANK), row),
                  pl.BlockSpec((tm, ROPE_PAD), row),
                  _const_spec(w_uk.shape), _const_spec(w_uv.shape)],
        out_specs=[pl.BlockSpec((tm, N_HEADS * HEAD_PAD), row),
                   pl.BlockSpec((tm, N_HEADS * V_DIM), row)],
        out_shape=[jax.ShapeDtypeStruct((n, N_HEADS * HEAD_PAD), BF16),
                   jax.ShapeDtypeStruct((n, N_HEADS * V_DIM), BF16)],
        compiler_params=_params("parallel"),
        name="kv_expand",
    )(ckv, kr_pad, w_uk, w_uv)


def _attn_kernel(*refs, n_kv):
    q_ref = refs[0]
    kv_refs = refs[1:1 + 2 * n_kv]
    o_ref = refs[1 + 2 * n_kv]
    scale = np.float32((NOPE_DIM + ROPE_DIM) ** -0.5)
    q = q_ref[0]
    scores = []
    for t in range(n_kv):
        k = kv_refs[2 * t][0]
        s = lax.dot_general(q, k, (((1,), (1,)), ((), ())),
                            preferred_element_type=F32)
        scores.append(s * scale)
    mx = functools.reduce(
        jnp.maximum, [jnp.max(s, axis=-1, keepdims=True) for s in scores])
    den = None
    acc = None
    for t in range(n_kv):
        p = jnp.exp(scores[t] - mx)
        ps = jnp.sum(p, axis=-1, keepdims=True)
        pv = _dot(p.astype(BF16), kv_refs[2 * t + 1][0])
        den = ps if den is None else den + ps
        acc = pv if acc is None else acc + pv
    o_ref[0] = (acc / den).astype(BF16)


def _attention(q, kvs, *, tq):
    b, t, _ = q.shape
    args = [q]
    specs = [pl.BlockSpec((1, tq, HEAD_PAD), lambda bi, h, qi: (bi, qi, h))]
    for k, v in kvs:
        s = k.shape[1]
        args += [k, v]
        specs += [pl.BlockSpec((1, s, HEAD_PAD), lambda bi, h, qi: (bi, 0, h)),
                  pl.BlockSpec((1, s, V_DIM), lambda bi, h, qi: (bi, 0, h))]
    return pl.pallas_call(
        functools.partial(_attn_kernel, n_kv=len(kvs)),
        grid=(b, N_HEADS, t // tq),
        in_specs=specs,
        out_specs=pl.BlockSpec((1, tq, V_DIM), lambda bi, h, qi: (bi, qi, h)),
        out_shape=jax.ShapeDtypeStruct((b, t, N_HEADS * V_DIM), BF16),
        compiler_params=_params("parallel", "parallel", "arbitrary"),
        name="attention",
    )(*args)


_ROPE_SWAP = np.concatenate([np.arange(16, 32), np.arange(0, 16),
                             np.arange(48, 64), np.arange(32, 48)])


def _mla_weights(w_dq, g_q, w_uq, w_dkv, g_kv, w_uk, w_uv):
    zpad = jnp.zeros((D_MODEL, ROPE_PAD - ROPE_DIM), F32)
    k_rope = w_dkv[:, KV_RANK:]
    dkv = jnp.concatenate(
        [w_dkv[:, :KV_RANK], k_rope, zpad, k_rope[:, _ROPE_SWAP], zpad], axis=1)
    uq = w_uq.reshape(Q_RANK, N_HEADS, NOPE_DIM + ROPE_DIM)
    q_rope = uq[:, :, NOPE_DIM:]
    hpad = jnp.zeros((Q_RANK, N_HEADS, ROPE_PAD - ROPE_DIM), F32)
    uq_full = jnp.concatenate([uq, hpad], axis=2).reshape(Q_RANK, N_HEADS * HEAD_PAD)
    uq_sw = jnp.concatenate([q_rope[:, :, _ROPE_SWAP], hpad], axis=2)
    uq_sw = uq_sw.reshape(Q_RANK, N_HEADS * ROPE_PAD)
    return dict(dkv=dkv.astype(BF16), g_kv=g_kv.reshape(1, -1),
                dq=w_dq.astype(BF16), g_q=g_q.reshape(1, -1),
                uq=uq_full.astype(BF16), uq_sw=uq_sw.astype(BF16),
                uk=w_uk.astype(BF16), uv=w_uv.astype(BF16))


def _rope_tables(t):
    half = ROPE_DIM // 4
    pos = jnp.arange(t, dtype=jnp.int32)
    r = (pos // GRID_W).astype(F32)
    col = (pos % GRID_W).astype(F32)
    inv = ROPE_BASE ** (-jnp.arange(half, dtype=F32) / half)
    ang_r = r[:, None] * inv[None, :]
    ang_c = col[:, None] * inv[None, :]
    zeros = jnp.zeros((t, ROPE_PAD - ROPE_DIM), F32)
    cos = jnp.concatenate([jnp.cos(ang_r), jnp.cos(ang_r), jnp.cos(ang_c),
                           jnp.cos(ang_c), zeros], axis=1)
    sin = jnp.concatenate([-jnp.sin(ang_r), jnp.sin(ang_r), -jnp.sin(ang_c),
                           jnp.sin(ang_c), zeros], axis=1)
    return cos, sin


def kernel(x_prompt, x_sample, cache_ckv, cache_krope, c, c_ctx, ada_w, ada_b, norm_g, gm_w_in, gm_b_in, gm_g_v, gm_w_s, gm_b_s, gm_w_out, mla_w_dq, mla_g_q, mla_w_uq, mla_w_dkv, mla_g_kv, mla_w_uk, mla_w_uv, mla_w_o, ffn_w_up, ffn_conv_w, ffn_conv_b, ffn_w_down):
    nb_ctx, t_ctx, d = x_prompt.shape
    nb_lat, t_lat, _ = x_sample.shape
    ctx_row = nb_lat

    cond = jnp.zeros((N_COND, d), F32).at[:nb_lat].set(c).at[ctx_row].set(c_ctx)
    mod = _ada(cond, ada_w, ada_b).reshape(ada_w.shape[0], N_COND, 6, d)

    mixer_w = (gm_w_in[0].astype(BF16), gm_b_in[0].reshape(1, -1),
               gm_g_v[0].reshape(1, -1), gm_w_s[0].astype(BF16), gm_b_s[0].T,
               gm_w_out[0].astype(BF16))
    ffn_w = [(ffn_w_up[i].astype(BF16), ffn_conv_w[i], ffn_conv_b[i].reshape(1, -1),
              ffn_w_down[i].astype(BF16)) for i in range(2)]
    mla_w = _mla_weights(mla_w_dq[0], mla_g_q[0], mla_w_uq[0], mla_w_dkv[0],
                         mla_g_kv[0], mla_w_uk[0], mla_w_uv[0])
    w_o = mla_w_o[0].astype(BF16)

    def trunk(x3, mod_row, tm_ffn, rope_tables, cached):
        nb, t, _ = x3.shape
        x = x3.reshape(nb * t, d)
        x = _mixer(x, mod[0], norm_g[0], *mixer_w, tm=256, mod_row=mod_row)
        x = _ffn(x, mod[0], norm_g[0], *ffn_w[0], tm=tm_ffn, seq_len=t,
                 mod_row=mod_row)
        proj = _mla_proj(x, mod[1], norm_g[1], mla_w, tm=min(512, t),
                         mod_row=mod_row, rope_tables=rope_tables, seq_len=t)
        q, k, v = (a.reshape(nb, t, -1) for a in proj[:3])
        kvs = [(k, v)]
        if cached is not None:
            kvs.append(cached)
        o = _attention(q, kvs, tq=min(512, t)).reshape(nb * t, -1)
        x = _ffn(x, mod[1], norm_g[1], *ffn_w[1], tm=tm_ffn, seq_len=t,
                 mod_row=mod_row, o=o, w_o=w_o)
        return x.reshape(nb, t, d), proj[3:]

    y_prompt, (ckv, krope) = trunk(x_prompt, lambda row: ctx_row, t_ctx, None, None)
    new_ckv = ckv.reshape(nb_ctx, 1, t_ctx, KV_RANK)
    new_krope = krope.reshape(nb_ctx, 1, t_ctx, ROPE_DIM)

    past = cache_ckv.shape[2]
    kr_pad = jnp.pad(cache_krope[:, 0].reshape(nb_lat * past, ROPE_DIM),
                     ((0, 0), (0, ROPE_PAD - ROPE_DIM)))
    k_c, v_c = _kv_expand(cache_ckv[:, 0].reshape(nb_lat * past, KV_RANK), kr_pad,
                          mla_w["uk"], mla_w["uv"], tm=512)
    cached = (k_c.reshape(nb_lat, past, -1), v_c.reshape(nb_lat, past, -1))
    y_sample, _ = trunk(x_sample, lambda row: row // t_lat, 1024,
                        _rope_tables(t_lat), cached)
    return (y_prompt, y_sample, new_ckv, new_krope)
```

```python
import functools

import jax
import jax.numpy as jnp
import numpy as np
from jax import lax
from jax.experimental import pallas as pl
from jax.experimental.pallas import tpu as pltpu

D_MODEL = 1024
GRID_W = 64
CHUNK = 128
D_INNER = 2 * D_MODEL
N_GROUPS = 8
GROUP_DIM = D_INNER // N_GROUPS
N_HEADS = 8
NOPE_DIM = 128
ROPE_DIM = 64
V_DIM = 128
Q_RANK = 384
KV_RANK = 256
ROPE_BASE = 10000.0
D_FF = 2816
EPS = 1e-6

LANES = 128
BF16_SUBLANES = 16
MXU_DIM = 256
VMEM_LIMIT_BYTES = 56 * 1024 * 1024

HEAD_PAD = 2 * LANES
ROPE_PAD = LANES
N_COND = 16
HALO = BF16_SUBLANES
FF_TILE = MXU_DIM
MAX_DOT_ROWS = 176
Q_CHUNK = MXU_DIM
KEY_CHUNK = 2 * MXU_DIM
QK_SCALE_LOG2E = float((NOPE_DIM + ROPE_DIM) ** -0.5 * np.log2(np.e))

F32 = jnp.float32
BF16 = jnp.bfloat16


def _dot(a, b):
    return jnp.dot(a, b, preferred_element_type=F32)


def _dot_nt(a, b):
    return lax.dot_general(a, b, (((1,), (1,)), ((), ())),
                           preferred_element_type=F32)


def _row_chunks(rows):
    n = -(-rows // MAX_DOT_ROWS)
    step = -(-rows // (n * BF16_SUBLANES)) * BF16_SUBLANES
    return [(lo, min(lo + step, rows)) for lo in range(0, rows, step)]


def _mdot(a, w):
    return jnp.concatenate(
        [_dot(a[lo:hi], w) for lo, hi in _row_chunks(a.shape[0])], axis=0)


def _rms(x, g):
    return x * lax.rsqrt(jnp.mean(x * x, axis=-1, keepdims=True) + EPS) * g


def _sigmoid(x):
    return 1.0 / (1.0 + jnp.exp(-x))


def _gelu(x):
    return 0.5 * x * (1.0 + lax.erf(x * np.float32(np.sqrt(0.5))))


def _params(*sem):
    return pltpu.CompilerParams(dimension_semantics=sem,
                                vmem_limit_bytes=VMEM_LIMIT_BYTES)


def _const_spec(shape):
    zeros = (0,) * len(shape)
    return pl.BlockSpec(shape, lambda *_: zeros, pipeline_mode=pl.Buffered(1))


def _ada_kernel(cond_ref, w_ref, b_ref, o_ref):
    c = cond_ref[...]
    s = (c * _sigmoid(c)).astype(BF16)
    o_ref[0] = _dot(s, w_ref[0].astype(BF16)) + b_ref[0]


def _ada(cond, ada_w, ada_b):
    depth, d, n = ada_w.shape
    tn = 512
    return pl.pallas_call(
        _ada_kernel,
        grid=(depth, n // tn),
        in_specs=[
            pl.BlockSpec((N_COND, d), lambda i, j: (0, 0)),
            pl.BlockSpec((1, d, tn), lambda i, j: (i, 0, j)),
            pl.BlockSpec((1, 1, tn), lambda i, j: (i, 0, j)),
        ],
        out_specs=pl.BlockSpec((1, N_COND, tn), lambda i, j: (i, 0, j)),
        out_shape=jax.ShapeDtypeStruct((depth, N_COND, n), F32),
        compiler_params=_params("parallel", "parallel"),
        name="ada",
    )(cond, ada_w, ada_b.reshape(depth, 1, n))


def _mixer_kernel(x_ref, mod_ref, ng_ref, w_in_ref, b_in_ref, gv_ref, ws_ref,
                  bs_ref, w_out_ref, o_ref, *, tm):
    m = mod_ref[0]
    ng = ng_ref[...]
    x = x_ref[...]
    hb = (_rms(x, ng[0:1]) * (1.0 + m[1:2]) + m[0:1]).astype(BF16)
    n_chunks = tm // CHUNK

    def pre(c):
        lo = c * GROUP_DIM
        return _mdot(hb, w_in_ref[:, lo:lo + GROUP_DIM]) + b_in_ref[:, lo:lo + GROUP_DIM]

    vs = []
    ss = None
    nxt = pre(N_GROUPS)
    for c in range(N_GROUPS):
        cur = nxt
        if c + 1 < N_GROUPS:
            nxt = pre(N_GROUPS + c + 1)
        v = _gelu(cur)
        s = jnp.sum(v * v, axis=-1, keepdims=True)
        ss = s if ss is None else ss + s
        vs.append(v)
    rs = lax.rsqrt(ss * (1.0 / D_INNER) + EPS)

    def mix(g):
        lo = g * GROUP_DIM
        vg = (vs[g] * rs * gv_ref[:, lo:lo + GROUP_DIM]).astype(BF16)
        w_s = ws_ref[g]
        return jnp.concatenate(
            [_dot(w_s, vg[n * CHUNK:(n + 1) * CHUNK]) for n in range(n_chunks)],
            axis=0)

    def down(t, g, acc):
        d = _mdot(t, w_out_ref[g * GROUP_DIM:(g + 1) * GROUP_DIM, :])
        return d if acc is None else acc + d

    acc = None
    t_prev = None
    nxt = (pre(0), mix(0))
    for g in range(N_GROUPS):
        u_pre, mixed = nxt
        if g + 1 < N_GROUPS:
            nxt = (pre(g + 1), mix(g + 1))
        if t_prev is not None:
            acc = down(t_prev, g - 1, acc)
        b_s = jnp.concatenate([bs_ref[:, g:g + 1]] * n_chunks, axis=0)
        t_prev = (_gelu(u_pre) * (mixed + b_s)).astype(BF16)
    acc = down(t_prev, N_GROUPS - 1, acc)
    o_ref[...] = x + m[2:3] * _rms(acc, ng[1:2])


def _mixer(x, mod, ng, w_in, b_in, g_v, w_s, b_s_t, w_out, *, tm, mod_row):
    n, d = x.shape
    return pl.pallas_call(
        functools.partial(_mixer_kernel, tm=tm),
        grid=(n // tm,),
        in_specs=[
            pl.BlockSpec((tm, d), lambda i: (i, 0)),
            pl.BlockSpec((1, 6, d), lambda i: (mod_row(i * tm), 0, 0)),
            _const_spec(ng.shape),
            _const_spec(w_in.shape),
            _const_spec(b_in.shape),
            _const_spec(g_v.shape),
            _const_spec(w_s.shape),
            _const_spec(b_s_t.shape),
            _const_spec(w_out.shape),
        ],
        out_specs=pl.BlockSpec((tm, d), lambda i: (i, 0)),
        out_shape=jax.ShapeDtypeStruct((n, d), F32),
        compiler_params=_params("parallel"),
        name="mixer",
    )(x, mod, ng, w_in, b_in, g_v, w_s, b_s_t, w_out)


def _ffn_kernel(*refs, tm, seq_len, fuse_oproj):
    if fuse_oproj:
        (x_ref, xp_ref, xn_ref, o_ref, op_ref, on_ref, wo_ref, *refs) = refs
    else:
        (x_ref, xp_ref, xn_ref, *refs) = refs
        o_ref = op_ref = on_ref = None
    (mod_ref, ng_ref, wup_ref, cw_ref, cb_ref, wd_ref, out_ref, hn_scr, x1_scr) = refs
    i = pl.program_id(0)
    m = mod_ref[0]
    ng = ng_ref[...]
    n_stages = D_FF // FF_TILE
    rows = tm + HALO

    def x1_of(xr, orf):
        if not fuse_oproj:
            return xr[...]
        return xr[...] + m[2:3] * _rms(_dot(orf[...], wo_ref[...]), ng[1:2])

    def hn_of(x1):
        return _rms(x1, ng[2:3]) * (1.0 + m[4:5]) + m[3:4]

    x1 = x1_of(x_ref, o_ref)
    x1_scr[...] = x1
    hn_scr[HALO:] = hn_of(x1).astype(BF16)
    row0 = i * tm
    at_start = (row0 % seq_len) == 0
    at_end = ((row0 + tm) % seq_len) == 0
    hp = hn_of(x1_of(xp_ref, op_ref))
    hp = jnp.where(at_start, jnp.zeros_like(hp), hp)
    hx = hn_of(x1_of(xn_ref, on_ref))
    hx = jnp.where(at_end, jnp.zeros_like(hx), hx)
    rid = lax.broadcasted_iota(jnp.int32, hp.shape, 0)
    hn_scr[0:HALO] = jnp.where(rid == 0, hx, hp).astype(BF16)
    hn = hn_scr[...]

    def up(j):
        lo = j * FF_TILE
        return (_mdot(hn, wup_ref[:, lo:lo + FF_TILE]),
                _mdot(hn, wup_ref[:, D_FF + lo:D_FF + lo + FF_TILE]))

    def conv(h, lo):
        cw = cw_ref[:, lo:lo + FF_TILE]
        return (pltpu.roll(h, 1, axis=0)[HALO:] * cw[0:1] + h[HALO:] * cw[1:2]
                + pltpu.roll(h, rows - 1, axis=0)[HALO:] * cw[2:3]
                + cb_ref[:, lo:lo + FF_TILE])

    def down(t, j, acc):
        d = _mdot(t, wd_ref[j * FF_TILE:(j + 1) * FF_TILE, :])
        return d if acc is None else acc + d

    nxt = up(0)
    acc = None
    t_prev = None
    for j in range(n_stages):
        hg, hv = nxt
        if j + 1 < n_stages:
            nxt = up(j + 1)
        if t_prev is not None:
            acc = down(t_prev, j - 1, acc)
        cg = conv(hg, j * FF_TILE)
        cv = conv(hv, D_FF + j * FF_TILE)
        t_prev = (cg * _sigmoid(cg) * cv).astype(BF16)
    acc = down(t_prev, n_stages - 1, acc)
    out_ref[...] = x1_scr[...] + m[5:6] * _rms(acc, ng[3:4])


def _ffn(x, mod, ng, w_up, conv_w, conv_b, w_down, *, tm, seq_len, mod_row,
         o=None, w_o=None):
    n, d = x.shape
    fuse = o is not None
    assert seq_len % tm == 0 and tm % HALO == 0
    hb = tm // HALO
    last = n // HALO - 1
    main = lambda i: (i, 0)
    prev = lambda i: (jnp.maximum(i * hb - 1, 0), 0)
    nxt = lambda i: (jnp.minimum((i + 1) * hb, last), 0)
    args = [x, x, x]
    specs = [pl.BlockSpec((tm, d), main), pl.BlockSpec((HALO, d), prev),
             pl.BlockSpec((HALO, d), nxt)]
    if fuse:
        args += [o, o, o, w_o]
        specs += [pl.BlockSpec((tm, d), main), pl.BlockSpec((HALO, d), prev),
                  pl.BlockSpec((HALO, d), nxt), _const_spec(w_o.shape)]
    consts = [ng, w_up, conv_w, conv_b, w_down]
    args += [mod] + consts
    specs += [pl.BlockSpec((1, 6, d), lambda i: (mod_row(i * tm), 0, 0))]
    specs += [_const_spec(a.shape) for a in consts]
    return pl.pallas_call(
        functools.partial(_ffn_kernel, tm=tm, seq_len=seq_len, fuse_oproj=fuse),
        grid=(n // tm,),
        in_specs=specs,
        out_specs=pl.BlockSpec((tm, d), main),
        out_shape=jax.ShapeDtypeStruct((n, d), F32),
        scratch_shapes=[pltpu.VMEM((tm + HALO, d), BF16), pltpu.VMEM((tm, d), F32)],
        compiler_params=_params("parallel"),
        name="ffn_oproj" if fuse else "ffn",
    )(*args)


def _store_keys(k_ref, kn, krb):
    for h in range(N_HEADS):
        lo = h * HEAD_PAD
        k_ref[:, lo:lo + NOPE_DIM] = kn[:, h * NOPE_DIM:(h + 1) * NOPE_DIM].astype(BF16)
        k_ref[:, lo + NOPE_DIM:lo + HEAD_PAD] = krb


def _store_values_t(vt_ref, wuvt_ref, ckvb):
    for lo, hi in _row_chunks(N_HEADS * V_DIM):
        vt_ref[0, lo:hi, :] = _dot_nt(wuvt_ref[lo:hi, :], ckvb).astype(BF16)


def _mla_proj_kernel(*refs, rope):
    (x_ref, mod_ref, ng_ref, wdkv_ref, gkv_ref, wdq_ref, gq_ref, wuq_ref,
     wuqs_ref, wuk_ref, wuvt_ref, *refs) = refs
    if rope:
        cos_ref, sin_ref, q_ref, k_ref, vt_ref = refs
        cos = cos_ref[...]
        sin = sin_ref[...]
    else:
        q_ref, k_ref, vt_ref, ckv_ref, kr_ref = refs
    m = mod_ref[0]
    ng = ng_ref[...]
    hb = (_rms(x_ref[...], ng[0:1]) * (1.0 + m[1:2]) + m[0:1]).astype(BF16)
    kv = _mdot(hb, wdkv_ref[...])
    ckv = _rms(kv[:, :KV_RANK], gkv_ref[...])
    kr = kv[:, KV_RANK:KV_RANK + ROPE_PAD]
    if rope:
        kr = kr * cos + kv[:, KV_RANK + ROPE_PAD:] * sin
    else:
        ckv_ref[...] = ckv
        kr_ref[...] = kv[:, KV_RANK:KV_RANK + ROPE_DIM]
    ql = _rms(_mdot(hb, wdq_ref[...]), gq_ref[...]).astype(BF16)
    q = _mdot(ql, wuq_ref[...])
    if rope:
        qs = _mdot(ql, wuqs_ref[...])
    ckvb = ckv.astype(BF16)
    _store_values_t(vt_ref, wuvt_ref, ckvb)
    _store_keys(k_ref, _mdot(ckvb, wuk_ref[...]), kr.astype(BF16))
    for h in range(N_HEADS):
        lo = h * HEAD_PAD
        mid = lo + NOPE_DIM
        q_ref[:, lo:mid] = (q[:, lo:mid] * QK_SCALE_LOG2E).astype(BF16)
        qr = q[:, mid:lo + HEAD_PAD]
        if rope:
            qr = qr * cos + qs[:, h * ROPE_PAD:(h + 1) * ROPE_PAD] * sin
        q_ref[:, mid:lo + HEAD_PAD] = (qr * QK_SCALE_LOG2E).astype(BF16)


def _mla_proj(x, mod, ng, w, *, tm, seq_len, mod_row, rope_tables=None):
    n, d = x.shape
    rope = rope_tables is not None
    tiles_per_seq = seq_len // tm
    consts = [ng, w["dkv"], w["g_kv"], w["dq"], w["g_q"], w["uq"], w["uq_sw"],
              w["uk"], w["uv_t"]]
    args = [x, mod] + consts
    specs = [pl.BlockSpec((tm, d), lambda i: (i, 0)),
             pl.BlockSpec((1, 6, d), lambda i: (mod_row(i * tm), 0, 0))]
    specs += [_const_spec(a.shape) for a in consts]
    row = lambda i: (i, 0)
    col = lambda i: (i // tiles_per_seq, 0, i % tiles_per_seq)
    out_shape = [jax.ShapeDtypeStruct((n, N_HEADS * HEAD_PAD), BF16),
                 jax.ShapeDtypeStruct((n, N_HEADS * HEAD_PAD), BF16),
                 jax.ShapeDtypeStruct((n // seq_len, N_HEADS * V_DIM, seq_len), BF16)]
    out_specs = [pl.BlockSpec((tm, N_HEADS * HEAD_PAD), row),
                 pl.BlockSpec((tm, N_HEADS * HEAD_PAD), row),
                 pl.BlockSpec((1, N_HEADS * V_DIM, tm), col)]
    if rope:
        args += list(rope_tables)
        specs += [pl.BlockSpec((tm, ROPE_PAD), lambda i: (i % tiles_per_seq, 0))] * 2
    else:
        out_shape += [jax.ShapeDtypeStruct((n, KV_RANK), F32),
                      jax.ShapeDtypeStruct((n, ROPE_DIM), F32)]
        out_specs += [pl.BlockSpec((tm, KV_RANK), row),
                      pl.BlockSpec((tm, ROPE_DIM), row)]
    return pl.pallas_call(
        functools.partial(_mla_proj_kernel, rope=rope),
        grid=(n // tm,),
        in_specs=specs,
        out_specs=out_specs,
        out_shape=out_shape,
        compiler_params=_params("parallel"),
        name="mla_proj_rope" if rope else "mla_proj",
    )(*args)


def _kv_expand_kernel(ckv_ref, kr_ref, wuk_ref, wuvt_ref, k_ref, vt_ref):
    ckvb = ckv_ref[0].astype(BF16)
    _store_values_t(vt_ref, wuvt_ref, ckvb)
    _store_keys(k_ref.at[0], _mdot(ckvb, wuk_ref[...]), kr_ref[0].astype(BF16))


def _kv_expand(ckv, kr_pad, w_uk, w_uv_t):
    b, s, _ = ckv.shape
    blk = lambda i: (i, 0, 0)
    return pl.pallas_call(
        _kv_expand_kernel,
        grid=(b,),
        in_specs=[pl.BlockSpec((1, s, KV_RANK), blk),
                  pl.BlockSpec((1, s, ROPE_PAD), blk),
                  _const_spec(w_uk.shape), _const_spec(w_uv_t.shape)],
        out_specs=[pl.BlockSpec((1, s, N_HEADS * HEAD_PAD), blk),
                   pl.BlockSpec((1, N_HEADS * V_DIM, s), blk)],
        out_shape=[jax.ShapeDtypeStruct((b, s, N_HEADS * HEAD_PAD), BF16),
                   jax.ShapeDtypeStruct((b, N_HEADS * V_DIM, s), BF16)],
        compiler_params=_params("parallel"),
        name="kv_expand",
    )(ckv, kr_pad, w_uk, w_uv_t)


def _attn_kernel(*refs, n_kv, heads, t):
    q_ref = refs[0]
    kv_refs = refs[1:1 + 2 * n_kv]
    o_ref = refs[1 + 2 * n_kv]
    items = [(h, c) for h in range(heads) for c in range(t // Q_CHUNK)]

    def key_blocks(s):
        n_keys = kv_refs[2 * s].shape[1]
        step = min(KEY_CHUNK, n_keys)
        return [(r, r + step) for r in range(0, n_keys, step)]

    def scores(item):
        h, c = item
        q = q_ref[0, c * Q_CHUNK:(c + 1) * Q_CHUNK, h * HEAD_PAD:(h + 1) * HEAD_PAD]
        return [_dot_nt(kv_refs[2 * s][0, lo:hi, h * HEAD_PAD:(h + 1) * HEAD_PAD], q)
                for s in range(n_kv) for lo, hi in key_blocks(s)]

    def softmax_pv(item, blocks):
        h, c = item
        mx = functools.reduce(
            jnp.maximum, [jnp.max(b, axis=0, keepdims=True) for b in blocks])
        ps = [jnp.exp2(b - mx) for b in blocks]
        den = functools.reduce(
            jnp.add, [jnp.sum(p, axis=0, keepdims=True) for p in ps])
        acc = None
        first = 0
        for s in range(n_kv):
            nb = len(key_blocks(s))
            pt = jnp.concatenate([p.astype(BF16) for p in ps[first:first + nb]], axis=0)
            first += nb
            d = _dot(kv_refs[2 * s + 1][0, h * V_DIM:(h + 1) * V_DIM, :], pt)
            acc = d if acc is None else acc + d
        o = (acc * (1.0 / den)).T
        o_ref[0, c * Q_CHUNK:(c + 1) * Q_CHUNK, h * V_DIM:(h + 1) * V_DIM] = o.astype(BF16)

    nxt = scores(items[0])
    for n, item in enumerate(items):
        cur = nxt
        if n + 1 < len(items):
            nxt = scores(items[n + 1])
        softmax_pv(item, cur)


def _attention(q, kvs, *, heads):
    b, t, _ = q.shape
    args = [q]
    specs = [pl.BlockSpec((1, t, heads * HEAD_PAD), lambda bi, h: (bi, 0, h))]
    for k, vt in kvs:
        s = k.shape[1]
        args += [k, vt]
        specs += [pl.BlockSpec((1, s, heads * HEAD_PAD), lambda bi, h: (bi, 0, h)),
                  pl.BlockSpec((1, heads * V_DIM, s), lambda bi, h: (bi, h, 0))]
    return pl.pallas_call(
        functools.partial(_attn_kernel, n_kv=len(kvs), heads=heads, t=t),
        grid=(b, N_HEADS // heads),
        in_specs=specs,
        out_specs=pl.BlockSpec((1, t, heads * V_DIM), lambda bi, h: (bi, 0, h)),
        out_shape=jax.ShapeDtypeStruct((b, t, N_HEADS * V_DIM), BF16),
        compiler_params=_params("parallel", "parallel"),
        name="attention",
    )(*args)


_ROPE_SWAP = np.concatenate([np.arange(16, 32), np.arange(0, 16),
                             np.arange(48, 64), np.arange(32, 48)])


def _mla_weights(w_dq, g_q, w_uq, w_dkv, g_kv, w_uk, w_uv):
    zpad = jnp.zeros((D_MODEL, ROPE_PAD - ROPE_DIM), F32)
    k_rope = w_dkv[:, KV_RANK:]
    dkv = jnp.concatenate(
        [w_dkv[:, :KV_RANK], k_rope, zpad, k_rope[:, _ROPE_SWAP], zpad], axis=1)
    uq = w_uq.reshape(Q_RANK, N_HEADS, NOPE_DIM + ROPE_DIM)
    q_rope = uq[:, :, NOPE_DIM:]
    hpad = jnp.zeros((Q_RANK, N_HEADS, ROPE_PAD - ROPE_DIM), F32)
    uq_full = jnp.concatenate([uq, hpad], axis=2).reshape(Q_RANK, N_HEADS * HEAD_PAD)
    uq_sw = jnp.concatenate([q_rope[:, :, _ROPE_SWAP], hpad], axis=2)
    uq_sw = uq_sw.reshape(Q_RANK, N_HEADS * ROPE_PAD)
    return dict(dkv=dkv.astype(BF16), g_kv=g_kv.reshape(1, -1),
                dq=w_dq.astype(BF16), g_q=g_q.reshape(1, -1),
                uq=uq_full.astype(BF16), uq_sw=uq_sw.astype(BF16),
                uk=w_uk.astype(BF16), uv_t=w_uv.T.astype(BF16))


def _rope_tables(t):
    half = ROPE_DIM // 4
    pos = jnp.arange(t, dtype=jnp.int32)
    r = (pos // GRID_W).astype(F32)
    col = (pos % GRID_W).astype(F32)
    inv = ROPE_BASE ** (-jnp.arange(half, dtype=F32) / half)
    ang_r = r[:, None] * inv[None, :]
    ang_c = col[:, None] * inv[None, :]
    zeros = jnp.zeros((t, ROPE_PAD - ROPE_DIM), F32)
    cos = jnp.concatenate([jnp.cos(ang_r), jnp.cos(ang_r), jnp.cos(ang_c),
                           jnp.cos(ang_c), zeros], axis=1)
    sin = jnp.concatenate([-jnp.sin(ang_r), jnp.sin(ang_r), -jnp.sin(ang_c),
                           jnp.sin(ang_c), zeros], axis=1)
    return cos, sin


def kernel(x_prompt, x_sample, cache_ckv, cache_krope, c, c_ctx, ada_w, ada_b, norm_g, gm_w_in, gm_b_in, gm_g_v, gm_w_s, gm_b_s, gm_w_out, mla_w_dq, mla_g_q, mla_w_uq, mla_w_dkv, mla_g_kv, mla_w_uk, mla_w_uv, mla_w_o, ffn_w_up, ffn_conv_w, ffn_conv_b, ffn_w_down):
    nb_ctx, t_ctx, d = x_prompt.shape
    nb_lat, t_lat, _ = x_sample.shape
    ctx_row = nb_lat

    cond = jnp.zeros((N_COND, d), F32).at[:nb_lat].set(c).at[ctx_row].set(c_ctx)
    mod = _ada(cond, ada_w, ada_b).reshape(ada_w.shape[0], N_COND, 6, d)

    mixer_w = (gm_w_in[0].astype(BF16), gm_b_in[0].reshape(1, -1),
               gm_g_v[0].reshape(1, -1), gm_w_s[0].astype(BF16), gm_b_s[0].T,
               gm_w_out[0].astype(BF16))
    ffn_w = [(ffn_w_up[i].astype(BF16), ffn_conv_w[i], ffn_conv_b[i].reshape(1, -1),
              ffn_w_down[i].astype(BF16)) for i in range(2)]
    mla_w = _mla_weights(mla_w_dq[0], mla_g_q[0], mla_w_uq[0], mla_w_dkv[0],
                         mla_g_kv[0], mla_w_uk[0], mla_w_uv[0])
    w_o = mla_w_o[0].astype(BF16)

    def trunk(x3, mod_row, heads, rope_tables, cached):
        nb, t, _ = x3.shape
        tm = min(512, t)
        x = x3.reshape(nb * t, d)
        x = _mixer(x, mod[0], norm_g[0], *mixer_w, tm=512, mod_row=mod_row)
        x = _ffn(x, mod[0], norm_g[0], *ffn_w[0], tm=tm, seq_len=t, mod_row=mod_row)
        proj = _mla_proj(x, mod[1], norm_g[1], mla_w, tm=tm, seq_len=t,
                         mod_row=mod_row, rope_tables=rope_tables)
        q, k = (a.reshape(nb, t, -1) for a in proj[:2])
        kvs = [(k, proj[2])]
        if cached is not None:
            kvs.append(cached)
        o = _attention(q, kvs, heads=heads).reshape(nb * t, -1)
        x = _ffn(x, mod[1], norm_g[1], *ffn_w[1], tm=tm, seq_len=t,
                 mod_row=mod_row, o=o, w_o=w_o)
        return x.reshape(nb, t, d), proj[3:]

    y_prompt, (ckv, krope) = trunk(x_prompt, lambda row: ctx_row, N_HEADS, None, None)
    new_ckv = ckv.reshape(nb_ctx, 1, t_ctx, KV_RANK)
    new_krope = krope.reshape(nb_ctx, 1, t_ctx, ROPE_DIM)

    kr_pad = jnp.pad(cache_krope[:, 0], ((0, 0), (0, 0), (0, ROPE_PAD - ROPE_DIM)))
    cached = _kv_expand(cache_ckv[:, 0], kr_pad, mla_w["uk"], mla_w["uv_t"])
    y_sample, _ = trunk(x_sample, lambda row: row // t_lat, 2, _rope_tables(t_lat),
                        cached)
    return (y_prompt, y_sample, new_ckv, new_krope)
```

```python
import functools

import jax
import jax.numpy as jnp
import numpy as np
from jax import lax
from jax.experimental import pallas as pl
from jax.experimental.pallas import tpu as pltpu

D_MODEL = 1024
GRID_W = 64
CHUNK = 128
D_INNER = 2 * D_MODEL
N_GROUPS = 8
GROUP_DIM = D_INNER // N_GROUPS
N_HEADS = 8
NOPE_DIM = 128
ROPE_DIM = 64
V_DIM = 128
Q_RANK = 384
KV_RANK = 256
ROPE_BASE = 10000.0
D_FF = 2816
EPS = 1e-6

LANES = 128
BF16_SUBLANES = 16
MXU_DIM = 256
VMEM_LIMIT_BYTES = 56 * 1024 * 1024

HEAD_PAD = 2 * LANES
ROPE_PAD = LANES
N_COND = 16
HALO = BF16_SUBLANES
FF_TILE = MXU_DIM
MAX_DOT_ROWS = 176
Q_CHUNK = MXU_DIM
KEY_CHUNK = 2 * MXU_DIM
QK_SCALE_LOG2E = float((NOPE_DIM + ROPE_DIM) ** -0.5 * np.log2(np.e))

F32 = jnp.float32
BF16 = jnp.bfloat16


def _dot(a, b):
    return jnp.dot(a, b, preferred_element_type=F32)


def _dot_nt(a, b):
    return lax.dot_general(a, b, (((1,), (1,)), ((), ())),
                           preferred_element_type=F32)


def _row_chunks(rows):
    n = -(-rows // MAX_DOT_ROWS)
    step = -(-rows // (n * BF16_SUBLANES)) * BF16_SUBLANES
    return [(lo, min(lo + step, rows)) for lo in range(0, rows, step)]


def _mdot(a, w):
    return jnp.concatenate(
        [_dot(a[lo:hi], w) for lo, hi in _row_chunks(a.shape[0])], axis=0)


def _rms(x, g):
    return x * lax.rsqrt(jnp.mean(x * x, axis=-1, keepdims=True) + EPS) * g


def _sigmoid(x):
    return 1.0 / (1.0 + jnp.exp(-x))


def _gelu(x):
    return 0.5 * x * (1.0 + lax.erf(x * np.float32(np.sqrt(0.5))))


def _params(*sem):
    return pltpu.CompilerParams(dimension_semantics=sem,
                                vmem_limit_bytes=VMEM_LIMIT_BYTES)


def _const_spec(shape):
    zeros = (0,) * len(shape)
    return pl.BlockSpec(shape, lambda *_: zeros, pipeline_mode=pl.Buffered(1))


def _ada_kernel(cond_ref, w_ref, b_ref, o_ref):
    c = cond_ref[...]
    s = (c * _sigmoid(c)).astype(BF16)
    o_ref[0] = _dot(s, w_ref[0].astype(BF16)) + b_ref[0]


def _ada(cond, ada_w, ada_b):
    depth, d, n = ada_w.shape
    tn = 2048
    return pl.pallas_call(
        _ada_kernel,
        grid=(depth, n // tn),
        in_specs=[
            pl.BlockSpec((N_COND, d), lambda i, j: (0, 0)),
            pl.BlockSpec((1, d, tn), lambda i, j: (i, 0, j)),
            pl.BlockSpec((1, 1, tn), lambda i, j: (i, 0, j)),
        ],
        out_specs=pl.BlockSpec((1, N_COND, tn), lambda i, j: (i, 0, j)),
        out_shape=jax.ShapeDtypeStruct((depth, N_COND, n), F32),
        compiler_params=_params("parallel", "parallel"),
        name="ada",
    )(cond, ada_w, ada_b.reshape(depth, 1, n))


def _mixer_kernel(x_ref, mod_ref, ng_ref, w_in_ref, b_in_ref, gv_ref, ws_ref,
                  bs_ref, w_out_ref, o_ref, *, tm):
    m = mod_ref[0]
    ng = ng_ref[...]
    x = x_ref[...]
    hb = (_rms(x, ng[0:1]) * (1.0 + m[1:2]) + m[0:1]).astype(BF16)
    n_chunks = tm // CHUNK

    def pre(c):
        lo = c * GROUP_DIM
        return _mdot(hb, w_in_ref[:, lo:lo + GROUP_DIM]) + b_in_ref[:, lo:lo + GROUP_DIM]

    vs = []
    ss = None
    nxt = pre(N_GROUPS)
    for c in range(N_GROUPS):
        cur = nxt
        if c + 1 < N_GROUPS:
            nxt = pre(N_GROUPS + c + 1)
        v = _gelu(cur)
        s = jnp.sum(v * v, axis=-1, keepdims=True)
        ss = s if ss is None else ss + s
        vs.append(v)
    rs = lax.rsqrt(ss * (1.0 / D_INNER) + EPS)

    def mix(g):
        lo = g * GROUP_DIM
        vg = (vs[g] * rs * gv_ref[:, lo:lo + GROUP_DIM]).astype(BF16)
        w_s = ws_ref[g]
        return jnp.concatenate(
            [_dot(w_s, vg[n * CHUNK:(n + 1) * CHUNK]) for n in range(n_chunks)],
            axis=0)

    def down(t, g, acc):
        d = _mdot(t, w_out_ref[g * GROUP_DIM:(g + 1) * GROUP_DIM, :])
        return d if acc is None else acc + d

    acc = None
    t_prev = None
    nxt = (pre(0), mix(0))
    for g in range(N_GROUPS):
        u_pre, mixed = nxt
        if g + 1 < N_GROUPS:
            nxt = (pre(g + 1), mix(g + 1))
        if t_prev is not None:
            acc = down(t_prev, g - 1, acc)
        b_s = jnp.concatenate([bs_ref[:, g:g + 1]] * n_chunks, axis=0)
        t_prev = (_gelu(u_pre) * (mixed + b_s)).astype(BF16)
    acc = down(t_prev, N_GROUPS - 1, acc)
    o_ref[...] = x + m[2:3] * _rms(acc, ng[1:2])


def _mixer(x, mod, ng, w_in, b_in, g_v, w_s, b_s_t, w_out, *, tm, mod_row):
    n, d = x.shape
    return pl.pallas_call(
        functools.partial(_mixer_kernel, tm=tm),
        grid=(n // tm,),
        in_specs=[
            pl.BlockSpec((tm, d), lambda i: (i, 0)),
            pl.BlockSpec((1, 6, d), lambda i: (mod_row(i * tm), 0, 0)),
            _const_spec(ng.shape),
            _const_spec(w_in.shape),
            _const_spec(b_in.shape),
            _const_spec(g_v.shape),
            _const_spec(w_s.shape),
            _const_spec(b_s_t.shape),
            _const_spec(w_out.shape),
        ],
        out_specs=pl.BlockSpec((tm, d), lambda i: (i, 0)),
        out_shape=jax.ShapeDtypeStruct((n, d), F32),
        compiler_params=_params("parallel"),
        name="mixer",
    )(x, mod, ng, w_in, b_in, g_v, w_s, b_s_t, w_out)


def _ffn_kernel(*refs, tm, seq_len, fuse_oproj):
    if fuse_oproj:
        (x_ref, xp_ref, xn_ref, o_ref, op_ref, on_ref, wo_ref, *refs) = refs
    else:
        (x_ref, xp_ref, xn_ref, *refs) = refs
        o_ref = op_ref = on_ref = None
    (mod_ref, ng_ref, wup_ref, cw_ref, cb_ref, wd_ref, out_ref, hn_scr, x1_scr) = refs
    i = pl.program_id(0)
    m = mod_ref[0]
    ng = ng_ref[...]
    n_stages = D_FF // FF_TILE
    rows = tm + HALO

    def x1_of(xr, orf):
        if not fuse_oproj:
            return xr[...]
        return xr[...] + m[2:3] * _rms(_dot(orf[...], wo_ref[...]), ng[1:2])

    def hn_of(x1):
        return _rms(x1, ng[2:3]) * (1.0 + m[4:5]) + m[3:4]

    x1 = x1_of(x_ref, o_ref)
    x1_scr[...] = x1
    hn_scr[HALO:] = hn_of(x1).astype(BF16)
    row0 = i * tm
    at_start = (row0 % seq_len) == 0
    at_end = ((row0 + tm) % seq_len) == 0
    hp = hn_of(x1_of(xp_ref, op_ref))
    hp = jnp.where(at_start, jnp.zeros_like(hp), hp)
    hx = hn_of(x1_of(xn_ref, on_ref))
    hx = jnp.where(at_end, jnp.zeros_like(hx), hx)
    rid = lax.broadcasted_iota(jnp.int32, hp.shape, 0)
    hn_scr[0:HALO] = jnp.where(rid == 0, hx, hp).astype(BF16)
    hn = hn_scr[...]

    def up(j):
        lo = j * FF_TILE
        return (_mdot(hn, wup_ref[:, lo:lo + FF_TILE]),
                _mdot(hn, wup_ref[:, D_FF + lo:D_FF + lo + FF_TILE]))

    def conv(h, lo):
        cw = cw_ref[:, lo:lo + FF_TILE]
        return (pltpu.roll(h, 1, axis=0)[HALO:] * cw[0:1] + h[HALO:] * cw[1:2]
                + pltpu.roll(h, rows - 1, axis=0)[HALO:] * cw[2:3]
                + cb_ref[:, lo:lo + FF_TILE])

    def down(t, j, acc):
        d = _mdot(t, wd_ref[j * FF_TILE:(j + 1) * FF_TILE, :])
        return d if acc is None else acc + d

    nxt = up(0)
    acc = None
    t_prev = None
    for j in range(n_stages):
        hg, hv = nxt
        if j + 1 < n_stages:
            nxt = up(j + 1)
        if t_prev is not None:
            acc = down(t_prev, j - 1, acc)
        cg = conv(hg, j * FF_TILE)
        cv = conv(hv, D_FF + j * FF_TILE)
        t_prev = (cg * _sigmoid(cg) * cv).astype(BF16)
    acc = down(t_prev, n_stages - 1, acc)
    out_ref[...] = x1_scr[...] + m[5:6] * _rms(acc, ng[3:4])


def _ffn(x, mod, ng, w_up, conv_w, conv_b, w_down, *, tm, seq_len, mod_row,
         o=None, w_o=None):
    n, d = x.shape
    fuse = o is not None
    assert seq_len % tm == 0 and tm % HALO == 0
    hb = tm // HALO
    last = n // HALO - 1
    main = lambda i: (i, 0)
    prev = lambda i: (jnp.maximum(i * hb - 1, 0), 0)
    nxt = lambda i: (jnp.minimum((i + 1) * hb, last), 0)
    args = [x, x, x]
    specs = [pl.BlockSpec((tm, d), main), pl.BlockSpec((HALO, d), prev),
             pl.BlockSpec((HALO, d), nxt)]
    if fuse:
        args += [o, o, o, w_o]
        specs += [pl.BlockSpec((tm, d), main), pl.BlockSpec((HALO, d), prev),
                  pl.BlockSpec((HALO, d), nxt), _const_spec(w_o.shape)]
    consts = [ng, w_up, conv_w, conv_b, w_down]
    args += [mod] + consts
    specs += [pl.BlockSpec((1, 6, d), lambda i: (mod_row(i * tm), 0, 0))]
    specs += [_const_spec(a.shape) for a in consts]
    return pl.pallas_call(
        functools.partial(_ffn_kernel, tm=tm, seq_len=seq_len, fuse_oproj=fuse),
        grid=(n // tm,),
        in_specs=specs,
        out_specs=pl.BlockSpec((tm, d), main),
        out_shape=jax.ShapeDtypeStruct((n, d), F32),
        scratch_shapes=[pltpu.VMEM((tm + HALO, d), BF16), pltpu.VMEM((tm, d), F32)],
        compiler_params=_params("parallel"),
        name="ffn_oproj" if fuse else "ffn",
    )(*args)


def _store_keys(k_ref, kn, krb):
    for h in range(N_HEADS):
        lo = h * HEAD_PAD
        k_ref[:, lo:lo + NOPE_DIM] = kn[:, h * NOPE_DIM:(h + 1) * NOPE_DIM].astype(BF16)
        k_ref[:, lo + NOPE_DIM:lo + HEAD_PAD] = krb


def _store_values_t(vt_ref, wuvt_ref, ckvb):
    for lo, hi in _row_chunks(N_HEADS * V_DIM):
        vt_ref[0, lo:hi, :] = _dot_nt(wuvt_ref[lo:hi, :], ckvb).astype(BF16)


def _mla_proj_kernel(*refs, rope):
    (x_ref, mod_ref, ng_ref, wdkv_ref, gkv_ref, wdq_ref, gq_ref, wuq_ref,
     wuk_ref, wuvt_ref, *refs) = refs
    if rope:
        cos_ref, sin_ref, q_ref, k_ref, vt_ref = refs
        cos = cos_ref[...]
        sin = sin_ref[...]
    else:
        q_ref, k_ref, vt_ref, ckv_ref, kr_ref = refs
    m = mod_ref[0]
    ng = ng_ref[...]
    hb = (_rms(x_ref[...], ng[0:1]) * (1.0 + m[1:2]) + m[0:1]).astype(BF16)
    kv = _mdot(hb, wdkv_ref[...])
    ckv = _rms(kv[:, :KV_RANK], gkv_ref[...])
    kr = kv[:, KV_RANK:KV_RANK + ROPE_PAD]
    if rope:
        kr = kr * cos + kv[:, KV_RANK + ROPE_PAD:] * sin
    else:
        ckv_ref[...] = ckv
        kr_ref[...] = kv[:, KV_RANK:KV_RANK + ROPE_DIM]
    ql = _rms(_mdot(hb, wdq_ref[...]), gq_ref[...]).astype(BF16)
    q = _mdot(ql, wuq_ref[...])
    ckvb = ckv.astype(BF16)
    _store_values_t(vt_ref, wuvt_ref, ckvb)
    _store_keys(k_ref, _mdot(ckvb, wuk_ref[...]), kr.astype(BF16))
    for h in range(N_HEADS):
        lo = h * HEAD_PAD
        mid = lo + NOPE_DIM
        q_ref[:, lo:mid] = (q[:, lo:mid] * QK_SCALE_LOG2E).astype(BF16)
        qr = q[:, mid:lo + HEAD_PAD]
        if rope:
            qr = qr * cos + pltpu.roll(qr, ROPE_DIM, axis=1) * sin
        q_ref[:, mid:lo + HEAD_PAD] = (qr * QK_SCALE_LOG2E).astype(BF16)


def _mla_proj(x, mod, ng, w, *, tm, seq_len, mod_row, rope_tables=None):
    n, d = x.shape
    rope = rope_tables is not None
    tiles_per_seq = seq_len // tm
    consts = [ng, w["dkv"], w["g_kv"], w["dq"], w["g_q"],
              w["uq_rope"] if rope else w["uq"], w["uk"], w["uv_t"]]
    args = [x, mod] + consts
    specs = [pl.BlockSpec((tm, d), lambda i: (i, 0)),
             pl.BlockSpec((1, 6, d), lambda i: (mod_row(i * tm), 0, 0))]
    specs += [_const_spec(a.shape) for a in consts]
    row = lambda i: (i, 0)
    col = lambda i: (i // tiles_per_seq, 0, i % tiles_per_seq)
    out_shape = [jax.ShapeDtypeStruct((n, N_HEADS * HEAD_PAD), BF16),
                 jax.ShapeDtypeStruct((n, N_HEADS * HEAD_PAD), BF16),
                 jax.ShapeDtypeStruct((n // seq_len, N_HEADS * V_DIM, seq_len), BF16)]
    out_specs = [pl.BlockSpec((tm, N_HEADS * HEAD_PAD), row),
                 pl.BlockSpec((tm, N_HEADS * HEAD_PAD), row),
                 pl.BlockSpec((1, N_HEADS * V_DIM, tm), col)]
    if rope:
        args += list(rope_tables)
        specs += [pl.BlockSpec((tm, ROPE_PAD), lambda i: (i % tiles_per_seq, 0))] * 2
    else:
        out_shape += [jax.ShapeDtypeStruct((n, KV_RANK), F32),
                      jax.ShapeDtypeStruct((n, ROPE_DIM), F32)]
        out_specs += [pl.BlockSpec((tm, KV_RANK), row),
                      pl.BlockSpec((tm, ROPE_DIM), row)]
    return pl.pallas_call(
        functools.partial(_mla_proj_kernel, rope=rope),
        grid=(n // tm,),
        in_specs=specs,
        out_specs=out_specs,
        out_shape=out_shape,
        compiler_params=_params("parallel"),
        name="mla_proj_rope" if rope else "mla_proj",
    )(*args)


def _kv_expand_kernel(ckv_ref, kr_ref, wuk_ref, wuvt_ref, k_ref, vt_ref):
    ckvb = ckv_ref[0].astype(BF16)
    _store_values_t(vt_ref, wuvt_ref, ckvb)
    _store_keys(k_ref.at[0], _mdot(ckvb, wuk_ref[...]), kr_ref[0].astype(BF16))


def _kv_expand(ckv, kr_pad, w_uk, w_uv_t):
    b, s, _ = ckv.shape
    blk = lambda i: (i, 0, 0)
    return pl.pallas_call(
        _kv_expand_kernel,
        grid=(b,),
        in_specs=[pl.BlockSpec((1, s, KV_RANK), blk),
                  pl.BlockSpec((1, s, ROPE_PAD), blk),
                  _const_spec(w_uk.shape), _const_spec(w_uv_t.shape)],
        out_specs=[pl.BlockSpec((1, s, N_HEADS * HEAD_PAD), blk),
                   pl.BlockSpec((1, N_HEADS * V_DIM, s), blk)],
        out_shape=[jax.ShapeDtypeStruct((b, s, N_HEADS * HEAD_PAD), BF16),
                   jax.ShapeDtypeStruct((b, N_HEADS * V_DIM, s), BF16)],
        compiler_params=_params("parallel"),
        name="kv_expand",
    )(ckv, kr_pad, w_uk, w_uv_t)


def _attn_kernel(*refs, n_kv, heads, t):
    q_ref = refs[0]
    kv_refs = refs[1:1 + 2 * n_kv]
    o_ref = refs[1 + 2 * n_kv]
    items = [(h, c) for h in range(heads) for c in range(t // Q_CHUNK)]

    def key_blocks(s):
        n_keys = kv_refs[2 * s].shape[1]
        step = min(KEY_CHUNK, n_keys)
        return [(r, r + step) for r in range(0, n_keys, step)]

    def scores(item):
        h, c = item
        q = q_ref[0, c * Q_CHUNK:(c + 1) * Q_CHUNK, h * HEAD_PAD:(h + 1) * HEAD_PAD]
        return [_dot_nt(kv_refs[2 * s][0, lo:hi, h * HEAD_PAD:(h + 1) * HEAD_PAD], q)
                for s in range(n_kv) for lo, hi in key_blocks(s)]

    def softmax_pv(item, blocks):
        h, c = item
        mx = functools.reduce(
            jnp.maximum, [jnp.max(b, axis=0, keepdims=True) for b in blocks])
        ps = [jnp.exp2(b - mx) for b in blocks]
        den = functools.reduce(
            jnp.add, [jnp.sum(p, axis=0, keepdims=True) for p in ps])
        acc = None
        first = 0
        for s in range(n_kv):
            nb = len(key_blocks(s))
            pt = jnp.concatenate([p.astype(BF16) for p in ps[first:first + nb]], axis=0)
            first += nb
            d = _dot(kv_refs[2 * s + 1][0, h * V_DIM:(h + 1) * V_DIM, :], pt)
            acc = d if acc is None else acc + d
        o = (acc * (1.0 / den)).T
        o_ref[0, c * Q_CHUNK:(c + 1) * Q_CHUNK, h * V_DIM:(h + 1) * V_DIM] = o.astype(BF16)

    nxt = scores(items[0])
    for n, item in enumerate(items):
        cur = nxt
        if n + 1 < len(items):
            nxt = scores(items[n + 1])
        softmax_pv(item, cur)


def _attention(q, kvs, *, heads):
    b, t, _ = q.shape
    args = [q]
    specs = [pl.BlockSpec((1, t, heads * HEAD_PAD), lambda bi, h: (bi, 0, h))]
    for k, vt in kvs:
        s = k.shape[1]
        args += [k, vt]
        specs += [pl.BlockSpec((1, s, heads * HEAD_PAD), lambda bi, h: (bi, 0, h)),
                  pl.BlockSpec((1, heads * V_DIM, s), lambda bi, h: (bi, h, 0))]
    return pl.pallas_call(
        functools.partial(_attn_kernel, n_kv=len(kvs), heads=heads, t=t),
        grid=(b, N_HEADS // heads),
        in_specs=specs,
        out_specs=pl.BlockSpec((1, t, heads * V_DIM), lambda bi, h: (bi, 0, h)),
        out_shape=jax.ShapeDtypeStruct((b, t, N_HEADS * V_DIM), BF16),
        compiler_params=_params("parallel", "parallel"),
        name="attention",
    )(*args)


_ROPE_SWAP = np.concatenate([np.arange(16, 32), np.arange(0, 16),
                             np.arange(48, 64), np.arange(32, 48)])


def _mla_weights(w_dq, g_q, w_uq, w_dkv, g_kv, w_uk, w_uv):
    zpad = jnp.zeros((D_MODEL, ROPE_PAD - ROPE_DIM), F32)
    k_rope = w_dkv[:, KV_RANK:]
    dkv = jnp.concatenate(
        [w_dkv[:, :KV_RANK], k_rope, zpad, k_rope[:, _ROPE_SWAP], zpad], axis=1)
    uq = w_uq.reshape(Q_RANK, N_HEADS, NOPE_DIM + ROPE_DIM)
    q_rope = uq[:, :, NOPE_DIM:]
    hpad = jnp.zeros((Q_RANK, N_HEADS, ROPE_PAD - ROPE_DIM), F32)
    uq_full = jnp.concatenate([uq, hpad], axis=2).reshape(Q_RANK, N_HEADS * HEAD_PAD)
    uq_rope = jnp.concatenate([uq, q_rope[:, :, _ROPE_SWAP]], axis=2)
    uq_rope = uq_rope.reshape(Q_RANK, N_HEADS * HEAD_PAD)
    return dict(dkv=dkv.astype(BF16), g_kv=g_kv.reshape(1, -1),
                dq=w_dq.astype(BF16), g_q=g_q.reshape(1, -1),
                uq=uq_full.astype(BF16), uq_rope=uq_rope.astype(BF16),
                uk=w_uk.astype(BF16), uv_t=w_uv.T.astype(BF16))


def _rope_tables(t):
    half = ROPE_DIM // 4
    pos = np.arange(t)
    inv = ROPE_BASE ** (-np.arange(half, dtype=np.float64) / half)
    ang_r = (pos // GRID_W)[:, None] * inv[None, :]
    ang_c = (pos % GRID_W)[:, None] * inv[None, :]
    zeros = np.zeros((t, ROPE_PAD - ROPE_DIM))
    cos = np.concatenate([np.cos(ang_r), np.cos(ang_r), np.cos(ang_c),
                          np.cos(ang_c), zeros], axis=1)
    sin = np.concatenate([-np.sin(ang_r), np.sin(ang_r), -np.sin(ang_c),
                          np.sin(ang_c), zeros], axis=1)
    return jnp.asarray(cos, F32), jnp.asarray(sin, F32)


def kernel(x_prompt, x_sample, cache_ckv, cache_krope, c, c_ctx, ada_w, ada_b, norm_g, gm_w_in, gm_b_in, gm_g_v, gm_w_s, gm_b_s, gm_w_out, mla_w_dq, mla_g_q, mla_w_uq, mla_w_dkv, mla_g_kv, mla_w_uk, mla_w_uv, mla_w_o, ffn_w_up, ffn_conv_w, ffn_conv_b, ffn_w_down):
    nb_ctx, t_ctx, d = x_prompt.shape
    nb_lat, t_lat, _ = x_sample.shape
    ctx_row = nb_lat

    cond = jnp.zeros((N_COND, d), F32).at[:nb_lat].set(c).at[ctx_row].set(c_ctx)
    mod = _ada(cond, ada_w, ada_b).reshape(ada_w.shape[0], N_COND, 6, d)

    mixer_w = (gm_w_in[0].astype(BF16), gm_b_in[0].reshape(1, -1),
               gm_g_v[0].reshape(1, -1), gm_w_s[0].astype(BF16), gm_b_s[0].T,
               gm_w_out[0].astype(BF16))
    ffn_w = [(ffn_w_up[i].astype(BF16), ffn_conv_w[i], ffn_conv_b[i].reshape(1, -1),
              ffn_w_down[i].astype(BF16)) for i in range(2)]
    mla_w = _mla_weights(mla_w_dq[0], mla_g_q[0], mla_w_uq[0], mla_w_dkv[0],
                         mla_g_kv[0], mla_w_uk[0], mla_w_uv[0])
    w_o = mla_w_o[0].astype(BF16)

    def trunk(x3, mod_row, heads, rope_tables, cached):
        nb, t, _ = x3.shape
        tm = min(512, t)
        x = x3.reshape(nb * t, d)
        x = _mixer(x, mod[0], norm_g[0], *mixer_w, tm=512, mod_row=mod_row)
        x = _ffn(x, mod[0], norm_g[0], *ffn_w[0], tm=tm, seq_len=t, mod_row=mod_row)
        proj = _mla_proj(x, mod[1], norm_g[1], mla_w, tm=tm, seq_len=t,
                         mod_row=mod_row, rope_tables=rope_tables)
        q, k = (a.reshape(nb, t, -1) for a in proj[:2])
        kvs = [(k, proj[2])]
        if cached is not None:
            kvs.append(cached)
        o = _attention(q, kvs, heads=heads).reshape(nb * t, -1)
        x = _ffn(x, mod[1], norm_g[1], *ffn_w[1], tm=tm, seq_len=t,
                 mod_row=mod_row, o=o, w_o=w_o)
        return x.reshape(nb, t, d), proj[3:]

    y_prompt, (ckv, krope) = trunk(x_prompt, lambda row: ctx_row, N_HEADS, None, None)
    new_ckv = ckv.reshape(nb_ctx, 1, t_ctx, KV_RANK)
    new_krope = krope.reshape(nb_ctx, 1, t_ctx, ROPE_DIM)

    kr_pad = jnp.pad(cache_krope[:, 0], ((0, 0), (0, 0), (0, ROPE_PAD - ROPE_DIM)))
    cached = _kv_expand(cache_ckv[:, 0], kr_pad, mla_w["uk"], mla_w["uv_t"])
    y_sample, _ = trunk(x_sample, lambda row: row // t_lat, 2, _rope_tables(t_lat),
                        cached)
    return (y_prompt, y_sample, new_ckv, new_krope)
```

```python
import functools

import jax
import jax.numpy as jnp
import numpy as np
from jax import lax
from jax.experimental import pallas as pl
from jax.experimental.pallas import tpu as pltpu

D_MODEL = 1024
GRID_W = 64
CHUNK = 128
D_INNER = 2 * D_MODEL
N_GROUPS = 8
GROUP_DIM = D_INNER // N_GROUPS
N_HEADS = 8
NOPE_DIM = 128
ROPE_DIM = 64
V_DIM = 128
Q_RANK = 384
KV_RANK = 256
ROPE_BASE = 10000.0
D_FF = 2816
EPS = 1e-6

LANES = 128
BF16_SUBLANES = 16
MXU_DIM = 256
VMEM_LIMIT_BYTES = 56 * 1024 * 1024

HEAD_PAD = 2 * LANES
ROPE_PAD = LANES
N_COND = 16
HALO = BF16_SUBLANES
FF_TILE = MXU_DIM
MAX_DOT_ROWS = 176
Q_CHUNK = MXU_DIM
KEY_CHUNK = 2 * MXU_DIM
QK_SCALE_LOG2E = float((NOPE_DIM + ROPE_DIM) ** -0.5 * np.log2(np.e))

F32 = jnp.float32
BF16 = jnp.bfloat16


def _dot(a, b):
    return jnp.dot(a, b, preferred_element_type=F32)


def _dot_nt(a, b):
    return lax.dot_general(a, b, (((1,), (1,)), ((), ())),
                           preferred_element_type=F32)


def _row_chunks(rows):
    n = -(-rows // MAX_DOT_ROWS)
    step = -(-rows // (n * BF16_SUBLANES)) * BF16_SUBLANES
    return [(lo, min(lo + step, rows)) for lo in range(0, rows, step)]


def _mdot(a, w):
    return jnp.concatenate(
        [_dot(a[lo:hi], w) for lo, hi in _row_chunks(a.shape[0])], axis=0)


def _rms(x, g):
    return x * lax.rsqrt(jnp.mean(x * x, axis=-1, keepdims=True) + EPS) * g


def _sigmoid(x):
    return 1.0 / (1.0 + jnp.exp(-x))


def _gelu(x):
    return 0.5 * x * (1.0 + lax.erf(x * np.float32(np.sqrt(0.5))))


def _params(*sem):
    return pltpu.CompilerParams(dimension_semantics=sem,
                                vmem_limit_bytes=VMEM_LIMIT_BYTES)


def _const_spec(shape):
    zeros = (0,) * len(shape)
    return pl.BlockSpec(shape, lambda *_: zeros, pipeline_mode=pl.Buffered(1))


def _ada_kernel(cond_ref, w_ref, b_ref, o_ref):
    c = cond_ref[...]
    s = (c * _sigmoid(c)).astype(BF16)
    o_ref[0] = _dot(s, w_ref[0].astype(BF16)) + b_ref[0]


def _ada(cond, ada_w, ada_b):
    depth, d, n = ada_w.shape
    tn = 2048
    return pl.pallas_call(
        _ada_kernel,
        grid=(depth, n // tn),
        in_specs=[
            pl.BlockSpec((N_COND, d), lambda i, j: (0, 0)),
            pl.BlockSpec((1, d, tn), lambda i, j: (i, 0, j)),
            pl.BlockSpec((1, 1, tn), lambda i, j: (i, 0, j)),
        ],
        out_specs=pl.BlockSpec((1, N_COND, tn), lambda i, j: (i, 0, j)),
        out_shape=jax.ShapeDtypeStruct((depth, N_COND, n), F32),
        compiler_params=_params("parallel", "parallel"),
        name="ada",
    )(cond, ada_w, ada_b.reshape(depth, 1, n))


def _mixer_tile(x_ref, o_ref, mod_ref, ng_ref, w_in_ref, b_in_ref, gv_ref, ws_ref,
                bs_ref, w_out_ref, *, tm):
    m = mod_ref[0]
    ng = ng_ref[...]
    x = x_ref[...]
    hb = (_rms(x, ng[0:1]) * (1.0 + m[1:2]) + m[0:1]).astype(BF16)
    n_chunks = tm // CHUNK

    def pre(c):
        lo = c * GROUP_DIM
        return _mdot(hb, w_in_ref[:, lo:lo + GROUP_DIM]) + b_in_ref[:, lo:lo + GROUP_DIM]

    vs = []
    ss = None
    nxt = pre(N_GROUPS)
    for c in range(N_GROUPS):
        cur = nxt
        if c + 1 < N_GROUPS:
            nxt = pre(N_GROUPS + c + 1)
        v = _gelu(cur)
        s = jnp.sum(v * v, axis=-1, keepdims=True)
        ss = s if ss is None else ss + s
        vs.append(v)
    rs = lax.rsqrt(ss * (1.0 / D_INNER) + EPS)

    def mix(g):
        lo = g * GROUP_DIM
        vg = (vs[g] * rs * gv_ref[:, lo:lo + GROUP_DIM]).astype(BF16)
        w_s = ws_ref[g]
        return jnp.concatenate(
            [_dot(w_s, vg[n * CHUNK:(n + 1) * CHUNK]) for n in range(n_chunks)],
            axis=0)

    def down(t, g, acc):
        d = _mdot(t, w_out_ref[g * GROUP_DIM:(g + 1) * GROUP_DIM, :])
        return d if acc is None else acc + d

    acc = None
    t_prev = None
    nxt = (pre(0), mix(0))
    for g in range(N_GROUPS):
        u_pre, mixed = nxt
        if g + 1 < N_GROUPS:
            nxt = (pre(g + 1), mix(g + 1))
        if t_prev is not None:
            acc = down(t_prev, g - 1, acc)
        b_s = jnp.concatenate([bs_ref[:, g:g + 1]] * n_chunks, axis=0)
        t_prev = (_gelu(u_pre) * (mixed + b_s)).astype(BF16)
    acc = down(t_prev, N_GROUPS - 1, acc)
    o_ref[...] = x + m[2:3] * _rms(acc, ng[1:2])


def _mixer_kernel(xc_ref, xl_ref, *refs, tm, n_ctx):
    *shared, oc_ref, ol_ref = refs
    i = pl.program_id(0)
    pl.when(i < n_ctx)(functools.partial(_mixer_tile, xc_ref, oc_ref, *shared, tm=tm))
    pl.when(i >= n_ctx)(functools.partial(_mixer_tile, xl_ref, ol_ref, *shared, tm=tm))


class _TwoGroups:
    def __init__(self, n_ctx, n_lat, tm_lat, t_lat, ctx_row):
        self.n_ctx, self.n_lat = n_ctx, n_lat
        self.tm_lat, self.t_lat, self.ctx_row = tm_lat, t_lat, ctx_row

    def ctx(self, i):
        return jnp.minimum(i, self.n_ctx - 1)

    def lat(self, i):
        return jnp.maximum(i - self.n_ctx, 0)

    def mod_spec(self, d):
        def index(i):
            lat_row = self.lat(i) * self.tm_lat // self.t_lat
            return (jnp.where(i < self.n_ctx, self.ctx_row, lat_row), 0, 0)
        return pl.BlockSpec((1, 6, d), index)


def _mixer(xc, xl, mod, ng, w_in, b_in, g_v, w_s, b_s_t, w_out, *, tm, t_lat, ctx_row):
    d = xc.shape[1]
    g = _TwoGroups(xc.shape[0] // tm, xl.shape[0] // tm, tm, t_lat, ctx_row)
    ctx_spec = pl.BlockSpec((tm, d), lambda i: (g.ctx(i), 0))
    lat_spec = pl.BlockSpec((tm, d), lambda i: (g.lat(i), 0))
    consts = [ng, w_in, b_in, g_v, w_s, b_s_t, w_out]
    return pl.pallas_call(
        functools.partial(_mixer_kernel, tm=tm, n_ctx=g.n_ctx),
        grid=(g.n_ctx + g.n_lat,),
        in_specs=[ctx_spec, lat_spec, g.mod_spec(d)] + [_const_spec(a.shape) for a in consts],
        out_specs=[ctx_spec, lat_spec],
        out_shape=[jax.ShapeDtypeStruct(xc.shape, F32), jax.ShapeDtypeStruct(xl.shape, F32)],
        compiler_params=_params("arbitrary"),
        name="mixer",
    )(xc, xl, mod, *consts)


def _ffn_tile(x_refs, o_refs, out_ref, tile, shared, *, tm, seq_len):
    (wo_ref, mod_ref, ng_ref, wup_ref, cw_ref, cb_ref, wd_ref, hn_scr, x1_scr) = shared
    fuse_oproj = o_refs is not None
    x_ref, xp_ref, xn_ref = x_refs
    o_ref, op_ref, on_ref = o_refs if fuse_oproj else (None, None, None)
    m = mod_ref[0]
    ng = ng_ref[...]
    n_stages = D_FF // FF_TILE
    rows = tm + HALO

    def x1_of(xr, orf):
        if not fuse_oproj:
            return xr[...]
        return xr[...] + m[2:3] * _rms(_dot(orf[...], wo_ref[...]), ng[1:2])

    def hn_of(x1):
        return _rms(x1, ng[2:3]) * (1.0 + m[4:5]) + m[3:4]

    x1 = x1_of(x_ref, o_ref)
    x1_scr[0:tm] = x1
    hn_scr[HALO:rows] = hn_of(x1).astype(BF16)
    row0 = tile * tm
    at_start = (row0 % seq_len) == 0
    at_end = ((row0 + tm) % seq_len) == 0
    hp = hn_of(x1_of(xp_ref, op_ref))
    hp = jnp.where(at_start, jnp.zeros_like(hp), hp)
    hx = hn_of(x1_of(xn_ref, on_ref))
    hx = jnp.where(at_end, jnp.zeros_like(hx), hx)
    rid = lax.broadcasted_iota(jnp.int32, hp.shape, 0)
    hn_scr[0:HALO] = jnp.where(rid == 0, hx, hp).astype(BF16)
    hn = hn_scr[0:rows]

    def up(j):
        lo = j * FF_TILE
        return (_mdot(hn, wup_ref[:, lo:lo + FF_TILE]),
                _mdot(hn, wup_ref[:, D_FF + lo:D_FF + lo + FF_TILE]))

    def conv(h, lo):
        cw = cw_ref[:, lo:lo + FF_TILE]
        return (pltpu.roll(h, 1, axis=0)[HALO:] * cw[0:1] + h[HALO:] * cw[1:2]
                + pltpu.roll(h, rows - 1, axis=0)[HALO:] * cw[2:3]
                + cb_ref[:, lo:lo + FF_TILE])

    def down(t, j, acc):
        d = _mdot(t, wd_ref[j * FF_TILE:(j + 1) * FF_TILE, :])
        return d if acc is None else acc + d

    nxt = up(0)
    acc = None
    t_prev = None
    for j in range(n_stages):
        hg, hv = nxt
        if j + 1 < n_stages:
            nxt = up(j + 1)
        if t_prev is not None:
            acc = down(t_prev, j - 1, acc)
        cg = conv(hg, j * FF_TILE)
        cv = conv(hv, D_FF + j * FF_TILE)
        t_prev = (cg * _sigmoid(cg) * cv).astype(BF16)
    acc = down(t_prev, n_stages - 1, acc)
    out_ref[...] = x1_scr[0:tm] + m[5:6] * _rms(acc, ng[3:4])


def _ffn_kernel(*refs, n_ctx, tm_ctx, t_ctx, tm_lat, t_lat, fuse_oproj):
    per_group = 6 if fuse_oproj else 3
    ctx_in, lat_in = refs[:per_group], refs[per_group:2 * per_group]
    refs = refs[2 * per_group:]
    if not fuse_oproj:
        refs = (None,) + refs
    *weights, outc_ref, outl_ref, hn_scr, x1_scr = refs
    shared = (*weights, hn_scr, x1_scr)
    i = pl.program_id(0)

    def run(group_in, out_ref, tile, tm, seq_len):
        o_refs = group_in[3:] if fuse_oproj else None
        _ffn_tile(group_in[:3], o_refs, out_ref, tile, shared, tm=tm, seq_len=seq_len)

    pl.when(i < n_ctx)(
        functools.partial(run, ctx_in, outc_ref, i, tm_ctx, t_ctx))
    pl.when(i >= n_ctx)(
        functools.partial(run, lat_in, outl_ref, i - n_ctx, tm_lat, t_lat))


def _ffn(xc, xl, mod, ng, w_up, conv_w, conv_b, w_down, *, t_ctx, t_lat, ctx_row,
         oc=None, ol=None, w_o=None):
    d = xc.shape[1]
    fuse = oc is not None
    tm_ctx, tm_lat = min(512, t_ctx), min(512, t_lat)
    g = _TwoGroups(xc.shape[0] // tm_ctx, xl.shape[0] // tm_lat, tm_lat, t_lat, ctx_row)

    def group_specs(n_rows, tm, tile):
        hb = tm // HALO
        last = n_rows // HALO - 1
        return [pl.BlockSpec((tm, d), lambda i: (tile(i), 0)),
                pl.BlockSpec((HALO, d), lambda i: (jnp.maximum(tile(i) * hb - 1, 0), 0)),
                pl.BlockSpec((HALO, d), lambda i: (jnp.minimum((tile(i) + 1) * hb, last), 0))]

    args, specs = [], []
    for x, o, tm, tile in ((xc, oc, tm_ctx, g.ctx), (xl, ol, tm_lat, g.lat)):
        for a in ([x, o] if fuse else [x]):
            args += [a, a, a]
            specs += group_specs(x.shape[0], tm, tile)
    consts = ([w_o] if fuse else []) + [mod, ng, w_up, conv_w, conv_b, w_down]
    args += consts
    specs += [g.mod_spec(d) if a is mod else _const_spec(a.shape) for a in consts]
    return pl.pallas_call(
        functools.partial(_ffn_kernel, n_ctx=g.n_ctx, tm_ctx=tm_ctx, t_ctx=t_ctx,
                          tm_lat=tm_lat, t_lat=t_lat, fuse_oproj=fuse),
        grid=(g.n_ctx + g.n_lat,),
        in_specs=specs,
        out_specs=[pl.BlockSpec((tm_ctx, d), lambda i: (g.ctx(i), 0)),
                   pl.BlockSpec((tm_lat, d), lambda i: (g.lat(i), 0))],
        out_shape=[jax.ShapeDtypeStruct(xc.shape, F32), jax.ShapeDtypeStruct(xl.shape, F32)],
        scratch_shapes=[pltpu.VMEM((max(tm_ctx, tm_lat) + HALO, d), BF16),
                        pltpu.VMEM((max(tm_ctx, tm_lat), d), F32)],
        compiler_params=_params("arbitrary"),
        name="ffn_oproj" if fuse else "ffn",
    )(*args)


def _store_keys(k_ref, kn, krb):
    for h in range(N_HEADS):
        lo = h * HEAD_PAD
        k_ref[:, lo:lo + NOPE_DIM] = kn[:, h * NOPE_DIM:(h + 1) * NOPE_DIM].astype(BF16)
        k_ref[:, lo + NOPE_DIM:lo + HEAD_PAD] = krb


def _store_values_t(vt_ref, wuvt_ref, ckvb):
    for lo, hi in _row_chunks(N_HEADS * V_DIM):
        vt_ref[0, lo:hi, :] = _dot_nt(wuvt_ref[lo:hi, :], ckvb).astype(BF16)


def _mla_proj_kernel(*refs, rope):
    (x_ref, mod_ref, ng_ref, wdkv_ref, gkv_ref, wdq_ref, gq_ref, wuq_ref,
     wuk_ref, wuvt_ref, *refs) = refs
    if rope:
        cos_ref, sin_ref, q_ref, k_ref, vt_ref = refs
        cos = cos_ref[...]
        sin = sin_ref[...]
    else:
        q_ref, k_ref, vt_ref, ckv_ref, kr_ref = refs
    m = mod_ref[0]
    ng = ng_ref[...]
    hb = (_rms(x_ref[...], ng[0:1]) * (1.0 + m[1:2]) + m[0:1]).astype(BF16)
    kv = _mdot(hb, wdkv_ref[...])
    ckv = _rms(kv[:, :KV_RANK], gkv_ref[...])
    kr = kv[:, KV_RANK:KV_RANK + ROPE_PAD]
    if rope:
        kr = kr * cos + kv[:, KV_RANK + ROPE_PAD:] * sin
    else:
        ckv_ref[...] = ckv
        kr_ref[...] = kv[:, KV_RANK:KV_RANK + ROPE_DIM]
    ql = _rms(_mdot(hb, wdq_ref[...]), gq_ref[...]).astype(BF16)
    q = _mdot(ql, wuq_ref[...])
    ckvb = ckv.astype(BF16)
    _store_values_t(vt_ref, wuvt_ref, ckvb)
    _store_keys(k_ref, _mdot(ckvb, wuk_ref[...]), kr.astype(BF16))
    for h in range(N_HEADS):
        lo = h * HEAD_PAD
        mid = lo + NOPE_DIM
        q_ref[:, lo:mid] = (q[:, lo:mid] * QK_SCALE_LOG2E).astype(BF16)
        qr = q[:, mid:lo + HEAD_PAD]
        if rope:
            qr = qr * cos + pltpu.roll(qr, ROPE_DIM, axis=1) * sin
        q_ref[:, mid:lo + HEAD_PAD] = (qr * QK_SCALE_LOG2E).astype(BF16)


def _mla_proj(x, mod, ng, w, *, tm, seq_len, mod_row, rope_tables=None):
    n, d = x.shape
    rope = rope_tables is not None
    tiles_per_seq = seq_len // tm
    consts = [ng, w["dkv"], w["g_kv"], w["dq"], w["g_q"],
              w["uq_rope"] if rope else w["uq"], w["uk"], w["uv_t"]]
    args = [x, mod] + consts
    specs = [pl.BlockSpec((tm, d), lambda i: (i, 0)),
             pl.BlockSpec((1, 6, d), lambda i: (mod_row(i * tm), 0, 0))]
    specs += [_const_spec(a.shape) for a in consts]
    row = lambda i: (i, 0)
    col = lambda i: (i // tiles_per_seq, 0, i % tiles_per_seq)
    out_shape = [jax.ShapeDtypeStruct((n, N_HEADS * HEAD_PAD), BF16),
                 jax.ShapeDtypeStruct((n, N_HEADS * HEAD_PAD), BF16),
                 jax.ShapeDtypeStruct((n // seq_len, N_HEADS * V_DIM, seq_len), BF16)]
    out_specs = [pl.BlockSpec((tm, N_HEADS * HEAD_PAD), row),
                 pl.BlockSpec((tm, N_HEADS * HEAD_PAD), row),
                 pl.BlockSpec((1, N_HEADS * V_DIM, tm), col)]
    if rope:
        args += list(rope_tables)
        specs += [pl.BlockSpec((tm, ROPE_PAD), lambda i: (i % tiles_per_seq, 0))] * 2
    else:
        out_shape += [jax.ShapeDtypeStruct((n, KV_RANK), F32),
                      jax.ShapeDtypeStruct((n, ROPE_DIM), F32)]
        out_specs += [pl.BlockSpec((tm, KV_RANK), row),
                      pl.BlockSpec((tm, ROPE_DIM), row)]
    return pl.pallas_call(
        functools.partial(_mla_proj_kernel, rope=rope),
        grid=(n // tm,),
        in_specs=specs,
        out_specs=out_specs,
        out_shape=out_shape,
        compiler_params=_params("parallel"),
        name="mla_proj_rope" if rope else "mla_proj",
    )(*args)


def _kv_expand_kernel(ckv_ref, kr_ref, wuk_ref, wuvt_ref, k_ref, vt_ref):
    ckvb = ckv_ref[0].astype(BF16)
    _store_values_t(vt_ref, wuvt_ref, ckvb)
    _store_keys(k_ref.at[0], _mdot(ckvb, wuk_ref[...]), kr_ref[0].astype(BF16))


def _kv_expand(ckv, kr_pad, w_uk, w_uv_t):
    b, s, _ = ckv.shape
    blk = lambda i: (i, 0, 0)
    return pl.pallas_call(
        _kv_expand_kernel,
        grid=(b,),
        in_specs=[pl.BlockSpec((1, s, KV_RANK), blk),
                  pl.BlockSpec((1, s, ROPE_PAD), blk),
                  _const_spec(w_uk.shape), _const_spec(w_uv_t.shape)],
        out_specs=[pl.BlockSpec((1, s, N_HEADS * HEAD_PAD), blk),
                   pl.BlockSpec((1, N_HEADS * V_DIM, s), blk)],
        out_shape=[jax.ShapeDtypeStruct((b, s, N_HEADS * HEAD_PAD), BF16),
                   jax.ShapeDtypeStruct((b, N_HEADS * V_DIM, s), BF16)],
        compiler_params=_params("parallel"),
        name="kv_expand",
    )(ckv, kr_pad, w_uk, w_uv_t)


def _attn_kernel(*refs, n_kv, heads, t):
    q_ref = refs[0]
    kv_refs = refs[1:1 + 2 * n_kv]
    o_ref = refs[1 + 2 * n_kv]
    items = [(h, c) for h in range(heads) for c in range(t // Q_CHUNK)]

    def key_blocks(s):
        n_keys = kv_refs[2 * s].shape[1]
        step = min(KEY_CHUNK, n_keys)
        return [(r, r + step) for r in range(0, n_keys, step)]

    def scores(item):
        h, c = item
        q = q_ref[0, c * Q_CHUNK:(c + 1) * Q_CHUNK, h * HEAD_PAD:(h + 1) * HEAD_PAD]
        return [_dot_nt(kv_refs[2 * s][0, lo:hi, h * HEAD_PAD:(h + 1) * HEAD_PAD], q)
                for s in range(n_kv) for lo, hi in key_blocks(s)]

    def softmax_pv(item, blocks):
        h, c = item
        mx = functools.reduce(
            jnp.maximum, [jnp.max(b, axis=0, keepdims=True) for b in blocks])
        ps = [jnp.exp2(b - mx) for b in blocks]
        den = functools.reduce(
            jnp.add, [jnp.sum(p, axis=0, keepdims=True) for p in ps])
        acc = None
        first = 0
        for s in range(n_kv):
            nb = len(key_blocks(s))
            pt = jnp.concatenate([p.astype(BF16) for p in ps[first:first + nb]], axis=0)
            first += nb
            d = _dot(kv_refs[2 * s + 1][0, h * V_DIM:(h + 1) * V_DIM, :], pt)
            acc = d if acc is None else acc + d
        o = (acc * (1.0 / den)).T
        o_ref[0, c * Q_CHUNK:(c + 1) * Q_CHUNK, h * V_DIM:(h + 1) * V_DIM] = o.astype(BF16)

    nxt = scores(items[0])
    for n, item in enumerate(items):
        cur = nxt
        if n + 1 < len(items):
            nxt = scores(items[n + 1])
        softmax_pv(item, cur)


def _attention(q, kvs, *, heads):
    b, t, _ = q.shape
    args = [q]
    specs = [pl.BlockSpec((1, t, heads * HEAD_PAD), lambda bi, h: (bi, 0, h))]
    for k, vt in kvs:
        s = k.shape[1]
        args += [k, vt]
        specs += [pl.BlockSpec((1, s, heads * HEAD_PAD), lambda bi, h: (bi, 0, h)),
                  pl.BlockSpec((1, heads * V_DIM, s), lambda bi, h: (bi, h, 0))]
    return pl.pallas_call(
        functools.partial(_attn_kernel, n_kv=len(kvs), heads=heads, t=t),
        grid=(b, N_HEADS // heads),
        in_specs=specs,
        out_specs=pl.BlockSpec((1, t, heads * V_DIM), lambda bi, h: (bi, 0, h)),
        out_shape=jax.ShapeDtypeStruct((b, t, N_HEADS * V_DIM), BF16),
        compiler_params=_params("parallel", "parallel"),
        name="attention",
    )(*args)


_ROPE_SWAP = np.concatenate([np.arange(16, 32), np.arange(0, 16),
                             np.arange(48, 64), np.arange(32, 48)])


def _mla_weights(w_dq, g_q, w_uq, w_dkv, g_kv, w_uk, w_uv):
    zpad = jnp.zeros((D_MODEL, ROPE_PAD - ROPE_DIM), F32)
    k_rope = w_dkv[:, KV_RANK:]
    dkv = jnp.concatenate(
        [w_dkv[:, :KV_RANK], k_rope, zpad, k_rope[:, _ROPE_SWAP], zpad], axis=1)
    uq = w_uq.reshape(Q_RANK, N_HEADS, NOPE_DIM + ROPE_DIM)
    q_rope = uq[:, :, NOPE_DIM:]
    hpad = jnp.zeros((Q_RANK, N_HEADS, ROPE_PAD - ROPE_DIM), F32)
    uq_full = jnp.concatenate([uq, hpad], axis=2).reshape(Q_RANK, N_HEADS * HEAD_PAD)
    uq_rope = jnp.concatenate([uq, q_rope[:, :, _ROPE_SWAP]], axis=2)
    uq_rope = uq_rope.reshape(Q_RANK, N_HEADS * HEAD_PAD)
    return dict(dkv=dkv.astype(BF16), g_kv=g_kv.reshape(1, -1),
                dq=w_dq.astype(BF16), g_q=g_q.reshape(1, -1),
                uq=uq_full.astype(BF16), uq_rope=uq_rope.astype(BF16),
                uk=w_uk.astype(BF16), uv_t=w_uv.T.astype(BF16))


def _rope_tables(t):
    half = ROPE_DIM // 4
    pos = np.arange(t)
    inv = ROPE_BASE ** (-np.arange(half, dtype=np.float64) / half)
    ang_r = (pos // GRID_W)[:, None] * inv[None, :]
    ang_c = (pos % GRID_W)[:, None] * inv[None, :]
    zeros = np.zeros((t, ROPE_PAD - ROPE_DIM))
    cos = np.concatenate([np.cos(ang_r), np.cos(ang_r), np.cos(ang_c),
                          np.cos(ang_c), zeros], axis=1)
    sin = np.concatenate([-np.sin(ang_r), np.sin(ang_r), -np.sin(ang_c),
                          np.sin(ang_c), zeros], axis=1)
    return jnp.asarray(cos, F32), jnp.asarray(sin, F32)


def kernel(x_prompt, x_sample, cache_ckv, cache_krope, c, c_ctx, ada_w, ada_b, norm_g, gm_w_in, gm_b_in, gm_g_v, gm_w_s, gm_b_s, gm_w_out, mla_w_dq, mla_g_q, mla_w_uq, mla_w_dkv, mla_g_kv, mla_w_uk, mla_w_uv, mla_w_o, ffn_w_up, ffn_conv_w, ffn_conv_b, ffn_w_down):
    nb_ctx, t_ctx, d = x_prompt.shape
    nb_lat, t_lat, _ = x_sample.shape
    ctx_row = nb_lat

    cond = jnp.zeros((N_COND, d), F32).at[:nb_lat].set(c).at[ctx_row].set(c_ctx)
    mod = _ada(cond, ada_w, ada_b).reshape(ada_w.shape[0], N_COND, 6, d)

    mixer_w = (gm_w_in[0].astype(BF16), gm_b_in[0].reshape(1, -1),
               gm_g_v[0].reshape(1, -1), gm_w_s[0].astype(BF16), gm_b_s[0].T,
               gm_w_out[0].astype(BF16))
    ffn_w = [(ffn_w_up[i].astype(BF16), ffn_conv_w[i], ffn_conv_b[i].reshape(1, -1),
              ffn_w_down[i].astype(BF16)) for i in range(2)]
    mla_w = _mla_weights(mla_w_dq[0], mla_g_q[0], mla_w_uq[0], mla_w_dkv[0],
                         mla_g_kv[0], mla_w_uk[0], mla_w_uv[0])
    w_o = mla_w_o[0].astype(BF16)

    groups = dict(t_ctx=t_ctx, t_lat=t_lat, ctx_row=ctx_row)
    xc = x_prompt.reshape(nb_ctx * t_ctx, d)
    xl = x_sample.reshape(nb_lat * t_lat, d)
    xc, xl = _mixer(xc, xl, mod[0], norm_g[0], *mixer_w, tm=512, t_lat=t_lat,
                    ctx_row=ctx_row)
    xc, xl = _ffn(xc, xl, mod[0], norm_g[0], *ffn_w[0], **groups)

    def attend(x, nb, t, mod_row, heads, rope_tables, cached):
        proj = _mla_proj(x, mod[1], norm_g[1], mla_w, tm=min(512, t), seq_len=t,
                         mod_row=mod_row, rope_tables=rope_tables)
        q, k = (a.reshape(nb, t, -1) for a in proj[:2])
        kvs = [(k, proj[2])] + ([cached] if cached is not None else [])
        return _attention(q, kvs, heads=heads).reshape(nb * t, -1), proj[3:]

    oc, (ckv, krope) = attend(xc, nb_ctx, t_ctx, lambda row: ctx_row, N_HEADS, None, None)
    kr_pad = jnp.pad(cache_krope[:, 0], ((0, 0), (0, 0), (0, ROPE_PAD - ROPE_DIM)))
    cached = _kv_expand(cache_ckv[:, 0], kr_pad, mla_w["uk"], mla_w["uv_t"])
    ol, _ = attend(xl, nb_lat, t_lat, lambda row: row // t_lat, 2, _rope_tables(t_lat),
                   cached)
    xc, xl = _ffn(xc, xl, mod[1], norm_g[1], *ffn_w[1], **groups, oc=oc, ol=ol, w_o=w_o)

    return (xc.reshape(nb_ctx, t_ctx, d), xl.reshape(nb_lat, t_lat, d),
            ckv.reshape(nb_ctx, 1, t_ctx, KV_RANK),
            krope.reshape(nb_ctx, 1, t_ctx, ROPE_DIM))
```

```python
import functools

import jax
import jax.numpy as jnp
import numpy as np
from jax import lax
from jax.experimental import pallas as pl
from jax.experimental.pallas import tpu as pltpu

D_MODEL = 1024
GRID_W = 64
CHUNK = 128
D_INNER = 2 * D_MODEL
N_GROUPS = 8
GROUP_DIM = D_INNER // N_GROUPS
N_HEADS = 8
NOPE_DIM = 128
ROPE_DIM = 64
V_DIM = 128
Q_RANK = 384
KV_RANK = 256
ROPE_BASE = 10000.0
D_FF = 2816
EPS = 1e-6

LANES = 128
BF16_SUBLANES = 16
MXU_DIM = 256
VMEM_LIMIT_BYTES = 56 * 1024 * 1024

HEAD_PAD = 2 * LANES
ROPE_PAD = LANES
N_COND = 16
HALO = BF16_SUBLANES
FF_TILE = MXU_DIM
MAX_DOT_ROWS = 176
CAST_STEPS = 32
Q_CHUNK = MXU_DIM
KEY_CHUNK = 2 * MXU_DIM
QK_SCALE_LOG2E = float((NOPE_DIM + ROPE_DIM) ** -0.5 * np.log2(np.e))

F32 = jnp.float32
BF16 = jnp.bfloat16


def _dot(a, b):
    return jnp.dot(a, b, preferred_element_type=F32)


def _dot_nt(a, b):
    return lax.dot_general(a, b, (((1,), (1,)), ((), ())),
                           preferred_element_type=F32)


def _row_chunks(rows):
    n = -(-rows // MAX_DOT_ROWS)
    step = -(-rows // (n * BF16_SUBLANES)) * BF16_SUBLANES
    return [(lo, min(lo + step, rows)) for lo in range(0, rows, step)]


def _mdot(a, w):
    return jnp.concatenate(
        [_dot(a[lo:hi], w) for lo, hi in _row_chunks(a.shape[0])], axis=0)


def _rms(x, g):
    return x * lax.rsqrt(jnp.mean(x * x, axis=-1, keepdims=True) + EPS) * g


def _sigmoid(x):
    return 1.0 / (1.0 + jnp.exp(-x))


def _gelu(x):
    return 0.5 * x * (1.0 + lax.erf(x * np.float32(np.sqrt(0.5))))


def _params(*sem):
    return pltpu.CompilerParams(dimension_semantics=sem,
                                vmem_limit_bytes=VMEM_LIMIT_BYTES)


def _const_spec(shape, layer=None):
    if layer is None:
        block, index = shape, (0,) * len(shape)
    else:
        block, index = (None,) + tuple(shape[1:]), (layer,) + (0,) * (len(shape) - 1)
    return pl.BlockSpec(block, lambda *_: index, pipeline_mode=pl.Buffered(1))


def _ada_kernel(cond_ref, w_ref, b_ref, o_ref):
    c = cond_ref[...]
    s = (c * _sigmoid(c)).astype(BF16)
    o_ref[0] = _dot(s, w_ref[0].astype(BF16)) + b_ref[0]


def _ada(cond, ada_w, ada_b):
    depth, d, n = ada_w.shape
    tn = 2048
    return pl.pallas_call(
        _ada_kernel,
        grid=(depth, n // tn),
        in_specs=[
            pl.BlockSpec((N_COND, d), lambda i, j: (0, 0)),
            pl.BlockSpec((1, d, tn), lambda i, j: (i, 0, j)),
            pl.BlockSpec((1, 1, tn), lambda i, j: (i, 0, j)),
        ],
        out_specs=pl.BlockSpec((1, N_COND, tn), lambda i, j: (i, 0, j)),
        out_shape=jax.ShapeDtypeStruct((depth, N_COND, n), F32),
        compiler_params=_params("parallel", "parallel"),
        name="ada",
    )(cond, ada_w, ada_b.reshape(depth, 1, n))


def _mixer_tile(x_ref, o_ref, mod_ref, ng_ref, w_in_ref, b_in_ref, gv_ref, ws_ref,
                bs_ref, w_out_ref, *, tm, casts):
    m = mod_ref[0]
    ng = ng_ref[...]
    x = x_ref[...]
    hb = (_rms(x, ng[0:1]) * (1.0 + m[1:2]) + m[0:1]).astype(BF16)
    n_chunks = tm // CHUNK

    def pre(c):
        lo = c * GROUP_DIM
        return _mdot(hb, w_in_ref[:, lo:lo + GROUP_DIM]) + b_in_ref[:, lo:lo + GROUP_DIM]

    vs = []
    ss = None
    nxt = pre(N_GROUPS)
    for c in range(N_GROUPS):
        cur = nxt
        if c + 1 < N_GROUPS:
            nxt = pre(N_GROUPS + c + 1)
        v = _gelu(cur)
        s = jnp.sum(v * v, axis=-1, keepdims=True)
        ss = s if ss is None else ss + s
        vs.append(v)
    rs = lax.rsqrt(ss * (1.0 / D_INNER) + EPS)

    for src_ref, dst_ref in casts:
        dst_ref[...] = src_ref[...].astype(BF16)

    def mix(g):
        lo = g * GROUP_DIM
        vg = (vs[g] * rs * gv_ref[:, lo:lo + GROUP_DIM]).astype(BF16)
        w_s = ws_ref[g]
        return jnp.concatenate(
            [_dot(w_s, vg[n * CHUNK:(n + 1) * CHUNK]) for n in range(n_chunks)],
            axis=0)

    def down(t, g, acc):
        d = _mdot(t, w_out_ref[g * GROUP_DIM:(g + 1) * GROUP_DIM, :])
        return d if acc is None else acc + d

    acc = None
    t_prev = None
    nxt = (pre(0), mix(0))
    for g in range(N_GROUPS):
        u_pre, mixed = nxt
        if g + 1 < N_GROUPS:
            nxt = (pre(g + 1), mix(g + 1))
        if t_prev is not None:
            acc = down(t_prev, g - 1, acc)
        b_s = jnp.concatenate([bs_ref[:, g:g + 1]] * n_chunks, axis=0)
        t_prev = (_gelu(u_pre) * (mixed + b_s)).astype(BF16)
    acc = down(t_prev, N_GROUPS - 1, acc)
    o_ref[...] = x + m[2:3] * _rms(acc, ng[1:2])


def _mixer_kernel(xc_ref, xl_ref, *refs, tm, n_ctx, n_cast):
    ins, (oc_ref, ol_ref, *cast_out) = refs[:-(2 + n_cast)], refs[-(2 + n_cast):]
    shared, cast_in = ins[:len(ins) - n_cast], ins[len(ins) - n_cast:]
    tile = functools.partial(_mixer_tile, tm=tm, casts=tuple(zip(cast_in, cast_out)))
    i = pl.program_id(0)
    pl.when(i < n_ctx)(functools.partial(tile, xc_ref, oc_ref, *shared))
    pl.when(i >= n_ctx)(functools.partial(tile, xl_ref, ol_ref, *shared))


class _TwoGroups:
    def __init__(self, n_ctx, n_lat, tm_lat, t_lat, ctx_row):
        self.n_ctx, self.n_lat = n_ctx, n_lat
        self.tm_lat, self.t_lat, self.ctx_row = tm_lat, t_lat, ctx_row

    def ctx(self, i):
        return jnp.minimum(i, self.n_ctx - 1)

    def lat(self, i):
        return jnp.maximum(i - self.n_ctx, 0)

    def mod_spec(self, d):
        def index(i):
            lat_row = self.lat(i) * self.tm_lat // self.t_lat
            return (jnp.where(i < self.n_ctx, self.ctx_row, lat_row), 0, 0)
        return pl.BlockSpec((1, 6, d), index)


def _mixer(xc, xl, mod, ng, w_in, b_in, g_v, w_s, b_s_t, w_out, *, tm, t_lat, ctx_row,
           casts):
    d = xc.shape[1]
    g = _TwoGroups(xc.shape[0] // tm, xl.shape[0] // tm, tm, t_lat, ctx_row)
    n_steps = g.n_ctx + g.n_lat
    ctx_spec = pl.BlockSpec((tm, d), lambda i: (g.ctx(i), 0))
    lat_spec = pl.BlockSpec((tm, d), lambda i: (g.lat(i), 0))
    consts = [ng, w_in, b_in, g_v, w_s, b_s_t, w_out]

    def cast_spec(a):
        rows, cols = a.shape
        assert rows % (CAST_STEPS * BF16_SUBLANES) == 0 and CAST_STEPS <= n_steps
        return pl.BlockSpec((rows // CAST_STEPS, cols),
                            lambda i: (jnp.minimum(i, CAST_STEPS - 1), 0))

    cast_specs = [cast_spec(a) for a in casts]
    return pl.pallas_call(
        functools.partial(_mixer_kernel, tm=tm, n_ctx=g.n_ctx, n_cast=len(casts)),
        grid=(n_steps,),
        in_specs=([ctx_spec, lat_spec, g.mod_spec(d)]
                  + [_const_spec(a.shape) for a in consts] + cast_specs),
        out_specs=[ctx_spec, lat_spec] + cast_specs,
        out_shape=([jax.ShapeDtypeStruct(xc.shape, F32), jax.ShapeDtypeStruct(xl.shape, F32)]
                   + [jax.ShapeDtypeStruct(a.shape, BF16) for a in casts]),
        compiler_params=_params("arbitrary"),
        name="mixer",
    )(xc, xl, mod, *consts, *casts)


def _ffn_tile(x_refs, o_refs, out_ref, tile, shared, *, tm, seq_len):
    (wo_ref, mod_ref, ng_ref, wup_ref, cw_ref, cb_ref, wd_ref, hn_scr, x1_scr) = shared
    fuse_oproj = o_refs is not None
    x_ref, xp_ref, xn_ref = x_refs
    o_ref, op_ref, on_ref = o_refs if fuse_oproj else (None, None, None)
    m = mod_ref[0]
    ng = ng_ref[...]
    n_stages = D_FF // FF_TILE
    rows = tm + HALO

    def x1_of(xr, orf):
        if not fuse_oproj:
            return xr[...]
        return xr[...] + m[2:3] * _rms(_dot(orf[...], wo_ref[...]), ng[1:2])

    def hn_of(x1):
        return _rms(x1, ng[2:3]) * (1.0 + m[4:5]) + m[3:4]

    x1 = x1_of(x_ref, o_ref)
    x1_scr[0:tm] = x1
    hn_scr[HALO:rows] = hn_of(x1).astype(BF16)
    row0 = tile * tm
    at_start = (row0 % seq_len) == 0
    at_end = ((row0 + tm) % seq_len) == 0
    hp = hn_of(x1_of(xp_ref, op_ref))
    hp = jnp.where(at_start, jnp.zeros_like(hp), hp)
    hx = hn_of(x1_of(xn_ref, on_ref))
    hx = jnp.where(at_end, jnp.zeros_like(hx), hx)
    rid = lax.broadcasted_iota(jnp.int32, hp.shape, 0)
    hn_scr[0:HALO] = jnp.where(rid == 0, hx, hp).astype(BF16)
    hn = hn_scr[0:rows]

    def up(j):
        lo = j * FF_TILE
        return (_mdot(hn, wup_ref[:, lo:lo + FF_TILE]),
                _mdot(hn, wup_ref[:, D_FF + lo:D_FF + lo + FF_TILE]))

    def conv(h, lo):
        cw = cw_ref[:, lo:lo + FF_TILE]
        return (pltpu.roll(h, 1, axis=0)[HALO:] * cw[0:1] + h[HALO:] * cw[1:2]
                + pltpu.roll(h, rows - 1, axis=0)[HALO:] * cw[2:3]
                + cb_ref[:, lo:lo + FF_TILE])

    def down(t, j, acc):
        d = _mdot(t, wd_ref[j * FF_TILE:(j + 1) * FF_TILE, :])
        return d if acc is None else acc + d

    nxt = up(0)
    acc = None
    t_prev = None
    for j in range(n_stages):
        hg, hv = nxt
        if j + 1 < n_stages:
            nxt = up(j + 1)
        if t_prev is not None:
            acc = down(t_prev, j - 1, acc)
        cg = conv(hg, j * FF_TILE)
        cv = conv(hv, D_FF + j * FF_TILE)
        t_prev = (cg * _sigmoid(cg) * cv).astype(BF16)
    acc = down(t_prev, n_stages - 1, acc)
    out_ref[...] = x1_scr[0:tm] + m[5:6] * _rms(acc, ng[3:4])


def _ffn_kernel(*refs, n_ctx, tm_ctx, t_ctx, tm_lat, t_lat, fuse_oproj):
    per_group = 6 if fuse_oproj else 3
    ctx_in, lat_in = refs[:per_group], refs[per_group:2 * per_group]
    refs = refs[2 * per_group:]
    if not fuse_oproj:
        refs = (None,) + refs
    *weights, outc_ref, outl_ref, hn_scr, x1_scr = refs
    shared = (*weights, hn_scr, x1_scr)
    i = pl.program_id(0)

    def run(group_in, out_ref, tile, tm, seq_len):
        o_refs = group_in[3:] if fuse_oproj else None
        _ffn_tile(group_in[:3], o_refs, out_ref, tile, shared, tm=tm, seq_len=seq_len)

    pl.when(i < n_ctx)(
        functools.partial(run, ctx_in, outc_ref, i, tm_ctx, t_ctx))
    pl.when(i >= n_ctx)(
        functools.partial(run, lat_in, outl_ref, i - n_ctx, tm_lat, t_lat))


def _ffn(xc, xl, mod, ng, w_up, conv_w, conv_b, w_down, *, layer, t_ctx, t_lat, ctx_row,
         oc=None, ol=None, w_o=None):
    d = xc.shape[1]
    fuse = oc is not None
    tm_ctx, tm_lat = min(512, t_ctx), min(512, t_lat)
    g = _TwoGroups(xc.shape[0] // tm_ctx, xl.shape[0] // tm_lat, tm_lat, t_lat, ctx_row)

    def group_specs(n_rows, tm, tile):
        hb = tm // HALO
        last = n_rows // HALO - 1
        return [pl.BlockSpec((tm, d), lambda i: (tile(i), 0)),
                pl.BlockSpec((HALO, d), lambda i: (jnp.maximum(tile(i) * hb - 1, 0), 0)),
                pl.BlockSpec((HALO, d), lambda i: (jnp.minimum((tile(i) + 1) * hb, last), 0))]

    args, specs = [], []
    for x, o, tm, tile in ((xc, oc, tm_ctx, g.ctx), (xl, ol, tm_lat, g.lat)):
        for a in ([x, o] if fuse else [x]):
            args += [a, a, a]
            specs += group_specs(x.shape[0], tm, tile)
    consts = ([w_o] if fuse else []) + [mod, ng, w_up, conv_w, conv_b, w_down]
    args += consts
    for a in consts:
        if a is mod:
            specs.append(g.mod_spec(d))
        else:
            specs.append(_const_spec(a.shape, layer if a is w_up or a is w_down else None))
    return pl.pallas_call(
        functools.partial(_ffn_kernel, n_ctx=g.n_ctx, tm_ctx=tm_ctx, t_ctx=t_ctx,
                          tm_lat=tm_lat, t_lat=t_lat, fuse_oproj=fuse),
        grid=(g.n_ctx + g.n_lat,),
        in_specs=specs,
        out_specs=[pl.BlockSpec((tm_ctx, d), lambda i: (g.ctx(i), 0)),
                   pl.BlockSpec((tm_lat, d), lambda i: (g.lat(i), 0))],
        out_shape=[jax.ShapeDtypeStruct(xc.shape, F32), jax.ShapeDtypeStruct(xl.shape, F32)],
        scratch_shapes=[pltpu.VMEM((max(tm_ctx, tm_lat) + HALO, d), BF16),
                        pltpu.VMEM((max(tm_ctx, tm_lat), d), F32)],
        compiler_params=_params("arbitrary"),
        name="ffn_oproj" if fuse else "ffn",
    )(*args)


def _store_keys(k_ref, kn, krb):
    for h in range(N_HEADS):
        lo = h * HEAD_PAD
        k_ref[:, lo:lo + NOPE_DIM] = kn[:, h * NOPE_DIM:(h + 1) * NOPE_DIM].astype(BF16)
        k_ref[:, lo + NOPE_DIM:lo + HEAD_PAD] = krb


def _store_values_t(vt_ref, wuvt_ref, ckvb):
    for lo, hi in _row_chunks(N_HEADS * V_DIM):
        vt_ref[0, lo:hi, :] = _dot_nt(wuvt_ref[lo:hi, :], ckvb).astype(BF16)


def _mla_proj_kernel(*refs, rope):
    (x_ref, mod_ref, ng_ref, wdkv_ref, gkv_ref, wdq_ref, gq_ref, wuq_ref,
     wuk_ref, wuvt_ref, *refs) = refs
    if rope:
        cos_ref, sin_ref, q_ref, k_ref, vt_ref = refs
        cos = cos_ref[...]
        sin = sin_ref[...]
    else:
        q_ref, k_ref, vt_ref, ckv_ref, kr_ref = refs
    m = mod_ref[0]
    ng = ng_ref[...]
    hb = (_rms(x_ref[...], ng[0:1]) * (1.0 + m[1:2]) + m[0:1]).astype(BF16)
    kv = _mdot(hb, wdkv_ref[...])
    ckv = _rms(kv[:, :KV_RANK], gkv_ref[...])
    kr = kv[:, KV_RANK:KV_RANK + ROPE_PAD]
    if rope:
        kr = kr * cos + kv[:, KV_RANK + ROPE_PAD:] * sin
    else:
        ckv_ref[...] = ckv
        kr_ref[...] = kv[:, KV_RANK:KV_RANK + ROPE_DIM]
    ql = _rms(_mdot(hb, wdq_ref[...]), gq_ref[...]).astype(BF16)
    q = _mdot(ql, wuq_ref[...])
    ckvb = ckv.astype(BF16)
    _store_values_t(vt_ref, wuvt_ref, ckvb)
    _store_keys(k_ref, _mdot(ckvb, wuk_ref[...]), kr.astype(BF16))
    for h in range(N_HEADS):
        lo = h * HEAD_PAD
        mid = lo + NOPE_DIM
        q_ref[:, lo:mid] = (q[:, lo:mid] * QK_SCALE_LOG2E).astype(BF16)
        qr = q[:, mid:lo + HEAD_PAD]
        if rope:
            qr = qr * cos + pltpu.roll(qr, ROPE_DIM, axis=1) * sin
        q_ref[:, mid:lo + HEAD_PAD] = (qr * QK_SCALE_LOG2E).astype(BF16)


def _mla_proj(x, mod, ng, w, *, tm, seq_len, mod_row, rope_tables=None):
    n, d = x.shape
    rope = rope_tables is not None
    tiles_per_seq = seq_len // tm
    consts = [ng, w["dkv"], w["g_kv"], w["dq"], w["g_q"],
              w["uq_rope"] if rope else w["uq"], w["uk"], w["uv_t"]]
    args = [x, mod] + consts
    specs = [pl.BlockSpec((tm, d), lambda i: (i, 0)),
             pl.BlockSpec((1, 6, d), lambda i: (mod_row(i * tm), 0, 0))]
    specs += [_const_spec(a.shape) for a in consts]
    row = lambda i: (i, 0)
    col = lambda i: (i // tiles_per_seq, 0, i % tiles_per_seq)
    out_shape = [jax.ShapeDtypeStruct((n, N_HEADS * HEAD_PAD), BF16),
                 jax.ShapeDtypeStruct((n, N_HEADS * HEAD_PAD), BF16),
                 jax.ShapeDtypeStruct((n // seq_len, N_HEADS * V_DIM, seq_len), BF16)]
    out_specs = [pl.BlockSpec((tm, N_HEADS * HEAD_PAD), row),
                 pl.BlockSpec((tm, N_HEADS * HEAD_PAD), row),
                 pl.BlockSpec((1, N_HEADS * V_DIM, tm), col)]
    if rope:
        args += list(rope_tables)
        specs += [pl.BlockSpec((tm, ROPE_PAD), lambda i: (i % tiles_per_seq, 0))] * 2
    else:
        out_shape += [jax.ShapeDtypeStruct((n, KV_RANK), F32),
                      jax.ShapeDtypeStruct((n, ROPE_DIM), F32)]
        out_specs += [pl.BlockSpec((tm, KV_RANK), row),
                      pl.BlockSpec((tm, ROPE_DIM), row)]
    return pl.pallas_call(
        functools.partial(_mla_proj_kernel, rope=rope),
        grid=(n // tm,),
        in_specs=specs,
        out_specs=out_specs,
        out_shape=out_shape,
        compiler_params=_params("parallel"),
        name="mla_proj_rope" if rope else "mla_proj",
    )(*args)


def _kv_expand_kernel(ckv_ref, kr_ref, wuk_ref, wuvt_ref, k_ref, vt_ref):
    ckvb = ckv_ref[0].astype(BF16)
    _store_values_t(vt_ref, wuvt_ref, ckvb)
    _store_keys(k_ref.at[0], _mdot(ckvb, wuk_ref[...]), kr_ref[0].astype(BF16))


def _kv_expand(ckv, kr_pad, w_uk, w_uv_t):
    b, s, _ = ckv.shape
    blk = lambda i: (i, 0, 0)
    return pl.pallas_call(
        _kv_expand_kernel,
        grid=(b,),
        in_specs=[pl.BlockSpec((1, s, KV_RANK), blk),
                  pl.BlockSpec((1, s, ROPE_PAD), blk),
                  _const_spec(w_uk.shape), _const_spec(w_uv_t.shape)],
        out_specs=[pl.BlockSpec((1, s, N_HEADS * HEAD_PAD), blk),
                   pl.BlockSpec((1, N_HEADS * V_DIM, s), blk)],
        out_shape=[jax.ShapeDtypeStruct((b, s, N_HEADS * HEAD_PAD), BF16),
                   jax.ShapeDtypeStruct((b, N_HEADS * V_DIM, s), BF16)],
        compiler_params=_params("parallel"),
        name="kv_expand",
    )(ckv, kr_pad, w_uk, w_uv_t)


def _attn_kernel(*refs, n_kv, heads, t):
    q_ref = refs[0]
    kv_refs = refs[1:1 + 2 * n_kv]
    o_ref = refs[1 + 2 * n_kv]
    items = [(h, c) for h in range(heads) for c in range(t // Q_CHUNK)]

    def key_blocks(s):
        n_keys = kv_refs[2 * s].shape[1]
        step = min(KEY_CHUNK, n_keys)
        return [(r, r + step) for r in range(0, n_keys, step)]

    def scores(item):
        h, c = item
        q = q_ref[0, c * Q_CHUNK:(c + 1) * Q_CHUNK, h * HEAD_PAD:(h + 1) * HEAD_PAD]
        return [_dot_nt(kv_refs[2 * s][0, lo:hi, h * HEAD_PAD:(h + 1) * HEAD_PAD], q)
                for s in range(n_kv) for lo, hi in key_blocks(s)]

    def softmax_pv(item, blocks):
        h, c = item
        mx = functools.reduce(
            jnp.maximum, [jnp.max(b, axis=0, keepdims=True) for b in blocks])
        ps = [jnp.exp2(b - mx) for b in blocks]
        den = functools.reduce(
            jnp.add, [jnp.sum(p, axis=0, keepdims=True) for p in ps])
        acc = None
        first = 0
        for s in range(n_kv):
            nb = len(key_blocks(s))
            pt = jnp.concatenate([p.astype(BF16) for p in ps[first:first + nb]], axis=0)
            first += nb
            d = _dot(kv_refs[2 * s + 1][0, h * V_DIM:(h + 1) * V_DIM, :], pt)
            acc = d if acc is None else acc + d
        o = (acc * (1.0 / den)).T
        o_ref[0, c * Q_CHUNK:(c + 1) * Q_CHUNK, h * V_DIM:(h + 1) * V_DIM] = o.astype(BF16)

    nxt = scores(items[0])
    for n, item in enumerate(items):
        cur = nxt
        if n + 1 < len(items):
            nxt = scores(items[n + 1])
        softmax_pv(item, cur)


def _attention(q, kvs, *, heads):
    b, t, _ = q.shape
    args = [q]
    specs = [pl.BlockSpec((1, t, heads * HEAD_PAD), lambda bi, h: (bi, 0, h))]
    for k, vt in kvs:
        s = k.shape[1]
        args += [k, vt]
        specs += [pl.BlockSpec((1, s, heads * HEAD_PAD), lambda bi, h: (bi, 0, h)),
                  pl.BlockSpec((1, heads * V_DIM, s), lambda bi, h: (bi, h, 0))]
    return pl.pallas_call(
        functools.partial(_attn_kernel, n_kv=len(kvs), heads=heads, t=t),
        grid=(b, N_HEADS // heads),
        in_specs=specs,
        out_specs=pl.BlockSpec((1, t, heads * V_DIM), lambda bi, h: (bi, 0, h)),
        out_shape=jax.ShapeDtypeStruct((b, t, N_HEADS * V_DIM), BF16),
        compiler_params=_params("parallel", "parallel"),
        name="attention",
    )(*args)


_ROPE_SWAP = np.concatenate([np.arange(16, 32), np.arange(0, 16),
                             np.arange(48, 64), np.arange(32, 48)])


def _mla_weights(w_dq, g_q, w_uq, w_dkv, g_kv, w_uk, w_uv):
    zpad = jnp.zeros((D_MODEL, ROPE_PAD - ROPE_DIM), F32)
    k_rope = w_dkv[:, KV_RANK:]
    dkv = jnp.concatenate(
        [w_dkv[:, :KV_RANK], k_rope, zpad, k_rope[:, _ROPE_SWAP], zpad], axis=1)
    uq = w_uq.reshape(Q_RANK, N_HEADS, NOPE_DIM + ROPE_DIM)
    q_rope = uq[:, :, NOPE_DIM:]
    hpad = jnp.zeros((Q_RANK, N_HEADS, ROPE_PAD - ROPE_DIM), F32)
    uq_full = jnp.concatenate([uq, hpad], axis=2).reshape(Q_RANK, N_HEADS * HEAD_PAD)
    uq_rope = jnp.concatenate([uq, q_rope[:, :, _ROPE_SWAP]], axis=2)
    uq_rope = uq_rope.reshape(Q_RANK, N_HEADS * HEAD_PAD)
    return dict(dkv=dkv.astype(BF16), g_kv=g_kv.reshape(1, -1),
                dq=w_dq.astype(BF16), g_q=g_q.reshape(1, -1),
                uq=uq_full.astype(BF16), uq_rope=uq_rope.astype(BF16),
                uk=w_uk.astype(BF16), uv_t=w_uv.T.astype(BF16))


def _rope_tables(t):
    half = ROPE_DIM // 4
    pos = np.arange(t)
    inv = ROPE_BASE ** (-np.arange(half, dtype=np.float64) / half)
    ang_r = (pos // GRID_W)[:, None] * inv[None, :]
    ang_c = (pos % GRID_W)[:, None] * inv[None, :]
    zeros = np.zeros((t, ROPE_PAD - ROPE_DIM))
    cos = np.concatenate([np.cos(ang_r), np.cos(ang_r), np.cos(ang_c),
                          np.cos(ang_c), zeros], axis=1)
    sin = np.concatenate([-np.sin(ang_r), np.sin(ang_r), -np.sin(ang_c),
                          np.sin(ang_c), zeros], axis=1)
    return jnp.asarray(cos, F32), jnp.asarray(sin, F32)


def kernel(x_prompt, x_sample, cache_ckv, cache_krope, c, c_ctx, ada_w, ada_b, norm_g, gm_w_in, gm_b_in, gm_g_v, gm_w_s, gm_b_s, gm_w_out, mla_w_dq, mla_g_q, mla_w_uq, mla_w_dkv, mla_g_kv, mla_w_uk, mla_w_uv, mla_w_o, ffn_w_up, ffn_conv_w, ffn_conv_b, ffn_w_down):
    nb_ctx, t_ctx, d = x_prompt.shape
    nb_lat, t_lat, _ = x_sample.shape
    ctx_row = nb_lat

    cond = jnp.zeros((N_COND, d), F32).at[:nb_lat].set(c).at[ctx_row].set(c_ctx)
    mod = _ada(cond, ada_w, ada_b).reshape(ada_w.shape[0], N_COND, 6, d)

    mixer_w = (gm_w_in[0].astype(BF16), gm_b_in[0].reshape(1, -1),
               gm_g_v[0].reshape(1, -1), gm_w_s[0].astype(BF16), gm_b_s[0].T,
               gm_w_out[0].astype(BF16))
    mla_w = _mla_weights(mla_w_dq[0], mla_g_q[0], mla_w_uq[0], mla_w_dkv[0],
                         mla_g_kv[0], mla_w_uk[0], mla_w_uv[0])

    groups = dict(t_ctx=t_ctx, t_lat=t_lat, ctx_row=ctx_row)
    xc = x_prompt.reshape(nb_ctx * t_ctx, d)
    xl = x_sample.reshape(nb_lat * t_lat, d)
    depth, _, up_cols = ffn_w_up.shape
    xc, xl, w_up, w_down, w_o = _mixer(
        xc, xl, mod[0], norm_g[0], *mixer_w, tm=512, t_lat=t_lat, ctx_row=ctx_row,
        casts=[ffn_w_up.reshape(depth * d, up_cols), ffn_w_down.reshape(depth * D_FF, d),
               mla_w_o[0]])
    w_up = w_up.reshape(depth, d, up_cols)
    w_down = w_down.reshape(depth, D_FF, d)
    ffn_w = [(w_up, ffn_conv_w[i], ffn_conv_b[i].reshape(1, -1), w_down) for i in range(depth)]
    xc, xl = _ffn(xc, xl, mod[0], norm_g[0], *ffn_w[0], layer=0, **groups)

    def attend(x, nb, t, mod_row, heads, rope_tables, cached):
        proj = _mla_proj(x, mod[1], norm_g[1], mla_w, tm=min(512, t), seq_len=t,
                         mod_row=mod_row, rope_tables=rope_tables)
        q, k = (a.reshape(nb, t, -1) for a in proj[:2])
        kvs = [(k, proj[2])] + ([cached] if cached is not None else [])
        return _attention(q, kvs, heads=heads).reshape(nb * t, -1), proj[3:]

    oc, (ckv, krope) = attend(xc, nb_ctx, t_ctx, lambda row: ctx_row, N_HEADS, None, None)
    kr_pad = jnp.pad(cache_krope[:, 0], ((0, 0), (0, 0), (0, ROPE_PAD - ROPE_DIM)))
    cached = _kv_expand(cache_ckv[:, 0], kr_pad, mla_w["uk"], mla_w["uv_t"])
    ol, _ = attend(xl, nb_lat, t_lat, lambda row: row // t_lat, 2, _rope_tables(t_lat),
                   cached)
    xc, xl = _ffn(xc, xl, mod[1], norm_g[1], *ffn_w[1], layer=1, **groups,
                  oc=oc, ol=ol, w_o=w_o)

    return (xc.reshape(nb_ctx, t_ctx, d), xl.reshape(nb_lat, t_lat, d),
            ckv.reshape(nb_ctx, 1, t_ctx, KV_RANK),
            krope.reshape(nb_ctx, 1, t_ctx, ROPE_DIM))
```

```python
import functools

import jax
import jax.numpy as jnp
import numpy as np
from jax import lax
from jax.experimental import pallas as pl
from jax.experimental.pallas import tpu as pltpu

D_MODEL = 1024
GRID_W = 64
CHUNK = 128
D_INNER = 2 * D_MODEL
N_GROUPS = 8
GROUP_DIM = D_INNER // N_GROUPS
N_HEADS = 8
NOPE_DIM = 128
ROPE_DIM = 64
V_DIM = 128
Q_RANK = 384
KV_RANK = 256
ROPE_BASE = 10000.0
D_FF = 2816
EPS = 1e-6

LANES = 128
BF16_SUBLANES = 16
MXU_DIM = 256
VMEM_LIMIT_BYTES = 56 * 1024 * 1024

HEAD_PAD = 2 * LANES
ROPE_PAD = LANES
N_COND = 16
HALO = BF16_SUBLANES
FF_TILE = MXU_DIM
MAX_DOT_ROWS = 176
CAST_STEPS = 32
Q_CHUNK = MXU_DIM
KEY_CHUNK = 2 * MXU_DIM
QK_SCALE_LOG2E = float((NOPE_DIM + ROPE_DIM) ** -0.5 * np.log2(np.e))

F32 = jnp.float32
BF16 = jnp.bfloat16


def _dot(a, b):
    return jnp.dot(a, b, preferred_element_type=F32)


def _dot_nt(a, b):
    return lax.dot_general(a, b, (((1,), (1,)), ((), ())),
                           preferred_element_type=F32)


def _row_chunks(rows):
    n = -(-rows // MAX_DOT_ROWS)
    step = -(-rows // (n * BF16_SUBLANES)) * BF16_SUBLANES
    return [(lo, min(lo + step, rows)) for lo in range(0, rows, step)]


def _mdot(a, w):
    return jnp.concatenate(
        [_dot(a[lo:hi], w) for lo, hi in _row_chunks(a.shape[0])], axis=0)


def _rms(x, g):
    return x * lax.rsqrt(jnp.mean(x * x, axis=-1, keepdims=True) + EPS) * g


def _sigmoid(x):
    return 1.0 / (1.0 + jnp.exp(-x))


def _gelu(x):
    return 0.5 * x * (1.0 + lax.erf(x * np.float32(np.sqrt(0.5))))


def _params(*sem):
    return pltpu.CompilerParams(dimension_semantics=sem,
                                vmem_limit_bytes=VMEM_LIMIT_BYTES)


def _const_spec(shape, layer=None):
    if layer is None:
        block, index = shape, (0,) * len(shape)
    else:
        block, index = (None,) + tuple(shape[1:]), (layer,) + (0,) * (len(shape) - 1)
    return pl.BlockSpec(block, lambda *_: index, pipeline_mode=pl.Buffered(1))


def _ada_kernel(cond_ref, w_ref, b_ref, o_ref):
    c = cond_ref[...]
    s = (c * _sigmoid(c)).astype(BF16)
    o_ref[0] = _dot(s, w_ref[0].astype(BF16)) + b_ref[0]


def _ada(cond, ada_w, ada_b):
    depth, d, n = ada_w.shape
    tn = 2048
    return pl.pallas_call(
        _ada_kernel,
        grid=(depth, n // tn),
        in_specs=[
            pl.BlockSpec((N_COND, d), lambda i, j: (0, 0)),
            pl.BlockSpec((1, d, tn), lambda i, j: (i, 0, j)),
            pl.BlockSpec((1, 1, tn), lambda i, j: (i, 0, j)),
        ],
        out_specs=pl.BlockSpec((1, N_COND, tn), lambda i, j: (i, 0, j)),
        out_shape=jax.ShapeDtypeStruct((depth, N_COND, n), F32),
        compiler_params=_params("parallel", "parallel"),
        name="ada",
    )(cond, ada_w, ada_b.reshape(depth, 1, n))


def _mixer_tile(x_ref, o_ref, mod_ref, ng_ref, w_in_ref, b_in_ref, gv_ref, ws_ref,
                bs_ref, w_out_ref, *, tm, casts):
    m = mod_ref[0]
    ng = ng_ref[...]
    x = x_ref[...]
    hb = (_rms(x, ng[0:1]) * (1.0 + m[1:2]) + m[0:1]).astype(BF16)
    n_chunks = tm // CHUNK

    def pre(c):
        lo = c * GROUP_DIM
        return _mdot(hb, w_in_ref[:, lo:lo + GROUP_DIM]) + b_in_ref[:, lo:lo + GROUP_DIM]

    vs = []
    ss = None
    nxt = pre(N_GROUPS)
    for c in range(N_GROUPS):
        cur = nxt
        if c + 1 < N_GROUPS:
            nxt = pre(N_GROUPS + c + 1)
        v = _gelu(cur)
        s = jnp.sum(v * v, axis=-1, keepdims=True)
        ss = s if ss is None else ss + s
        vs.append(v)
    rs = lax.rsqrt(ss * (1.0 / D_INNER) + EPS)

    for src_ref, dst_ref in casts:
        dst_ref[...] = src_ref[...].astype(BF16)

    def mix(g):
        lo = g * GROUP_DIM
        vg = (vs[g] * rs * gv_ref[:, lo:lo + GROUP_DIM]).astype(BF16)
        w_s = ws_ref[g]
        return jnp.concatenate(
            [_dot(w_s, vg[n * CHUNK:(n + 1) * CHUNK]) for n in range(n_chunks)],
            axis=0)

    def down(t, g, acc):
        d = _mdot(t, w_out_ref[g * GROUP_DIM:(g + 1) * GROUP_DIM, :])
        return d if acc is None else acc + d

    acc = None
    t_prev = None
    nxt = (pre(0), mix(0))
    for g in range(N_GROUPS):
        u_pre, mixed = nxt
        if g + 1 < N_GROUPS:
            nxt = (pre(g + 1), mix(g + 1))
        if t_prev is not None:
            acc = down(t_prev, g - 1, acc)
        b_s = jnp.concatenate([bs_ref[:, g:g + 1]] * n_chunks, axis=0)
        t_prev = (_gelu(u_pre) * (mixed + b_s)).astype(BF16)
    acc = down(t_prev, N_GROUPS - 1, acc)
    o_ref[...] = x + m[2:3] * _rms(acc, ng[1:2])


def _mixer_kernel(xc_ref, xl_ref, *refs, tm, n_ctx, n_cast):
    ins, (oc_ref, ol_ref, *cast_out) = refs[:-(2 + n_cast)], refs[-(2 + n_cast):]
    shared, cast_in = ins[:len(ins) - n_cast], ins[len(ins) - n_cast:]
    tile = functools.partial(_mixer_tile, tm=tm, casts=tuple(zip(cast_in, cast_out)))
    i = pl.program_id(0)
    pl.when(i < n_ctx)(functools.partial(tile, xc_ref, oc_ref, *shared))
    pl.when(i >= n_ctx)(functools.partial(tile, xl_ref, ol_ref, *shared))


class _TwoGroups:
    def __init__(self, n_ctx, n_lat, tm_lat, t_lat, ctx_row):
        self.n_ctx, self.n_lat = n_ctx, n_lat
        self.tm_lat, self.t_lat, self.ctx_row = tm_lat, t_lat, ctx_row

    def ctx(self, i):
        return jnp.minimum(i, self.n_ctx - 1)

    def lat(self, i):
        return jnp.maximum(i - self.n_ctx, 0)

    def mod_spec(self, d):
        def index(i):
            lat_row = self.lat(i) * self.tm_lat // self.t_lat
            return (jnp.where(i < self.n_ctx, self.ctx_row, lat_row), 0, 0)
        return pl.BlockSpec((1, 6, d), index)


def _mixer(xc, xl, mod, ng, w_in, b_in, g_v, w_s, b_s_t, w_out, *, tm, t_lat, ctx_row,
           casts):
    d = xc.shape[1]
    g = _TwoGroups(xc.shape[0] // tm, xl.shape[0] // tm, tm, t_lat, ctx_row)
    n_steps = g.n_ctx + g.n_lat
    ctx_spec = pl.BlockSpec((tm, d), lambda i: (g.ctx(i), 0))
    lat_spec = pl.BlockSpec((tm, d), lambda i: (g.lat(i), 0))
    consts = [ng, w_in, b_in, g_v, w_s, b_s_t, w_out]

    def cast_spec(a):
        rows, cols = a.shape
        assert rows % (CAST_STEPS * BF16_SUBLANES) == 0 and CAST_STEPS <= n_steps
        return pl.BlockSpec((rows // CAST_STEPS, cols),
                            lambda i: (jnp.minimum(i, CAST_STEPS - 1), 0))

    cast_specs = [cast_spec(a) for a in casts]
    return pl.pallas_call(
        functools.partial(_mixer_kernel, tm=tm, n_ctx=g.n_ctx, n_cast=len(casts)),
        grid=(n_steps,),
        in_specs=([ctx_spec, lat_spec, g.mod_spec(d)]
                  + [_const_spec(a.shape) for a in consts] + cast_specs),
        out_specs=[ctx_spec, lat_spec] + cast_specs,
        out_shape=([jax.ShapeDtypeStruct(xc.shape, F32), jax.ShapeDtypeStruct(xl.shape, F32)]
                   + [jax.ShapeDtypeStruct(a.shape, BF16) for a in casts]),
        compiler_params=_params("arbitrary"),
        name="mixer",
    )(xc, xl, mod, *consts, *casts)


def _ffn_tile(x_refs, o_refs, out_ref, tile, shared, *, tm, seq_len):
    (wo_ref, mod_ref, ng_ref, wup_ref, cw_ref, cb_ref, wd_ref, hn_scr, x1_scr) = shared
    fuse_oproj = o_refs is not None
    x_ref, xp_ref, xn_ref = x_refs
    o_ref, op_ref, on_ref = o_refs if fuse_oproj else (None, None, None)
    m = mod_ref[0]
    ng = ng_ref[...]
    n_stages = D_FF // FF_TILE
    rows = tm + HALO

    def x1_of(xr, orf):
        if not fuse_oproj:
            return xr[...]
        return xr[...] + m[2:3] * _rms(_dot(orf[...], wo_ref[...]), ng[1:2])

    def hn_of(x1):
        return _rms(x1, ng[2:3]) * (1.0 + m[4:5]) + m[3:4]

    x1 = x1_of(x_ref, o_ref)
    x1_scr[0:tm] = x1
    hn_scr[HALO:rows] = hn_of(x1).astype(BF16)
    row0 = tile * tm
    at_start = (row0 % seq_len) == 0
    at_end = ((row0 + tm) % seq_len) == 0
    hp = hn_of(x1_of(xp_ref, op_ref))
    hp = jnp.where(at_start, jnp.zeros_like(hp), hp)
    hx = hn_of(x1_of(xn_ref, on_ref))
    hx = jnp.where(at_end, jnp.zeros_like(hx), hx)
    rid = lax.broadcasted_iota(jnp.int32, hp.shape, 0)
    hn_scr[0:HALO] = jnp.where(rid == 0, hx, hp).astype(BF16)
    hn = hn_scr[0:rows]

    def up(j):
        lo = j * FF_TILE
        return (_mdot(hn, wup_ref[:, lo:lo + FF_TILE]),
                _mdot(hn, wup_ref[:, D_FF + lo:D_FF + lo + FF_TILE]))

    def conv(h, lo):
        cw = cw_ref[:, lo:lo + FF_TILE]
        return (pltpu.roll(h, 1, axis=0)[HALO:] * cw[0:1] + h[HALO:] * cw[1:2]
                + pltpu.roll(h, rows - 1, axis=0)[HALO:] * cw[2:3]
                + cb_ref[:, lo:lo + FF_TILE])

    def down(t, j, acc):
        d = _mdot(t, wd_ref[j * FF_TILE:(j + 1) * FF_TILE, :])
        return d if acc is None else acc + d

    nxt = up(0)
    acc = None
    t_prev = None
    for j in range(n_stages):
        hg, hv = nxt
        if j + 1 < n_stages:
            nxt = up(j + 1)
        if t_prev is not None:
            acc = down(t_prev, j - 1, acc)
        cg = conv(hg, j * FF_TILE)
        cv = conv(hv, D_FF + j * FF_TILE)
        t_prev = (cg * _sigmoid(cg) * cv).astype(BF16)
    acc = down(t_prev, n_stages - 1, acc)
    out_ref[...] = x1_scr[0:tm] + m[5:6] * _rms(acc, ng[3:4])


def _ffn_kernel(*refs, n_ctx, tm_ctx, t_ctx, tm_lat, t_lat, fuse_oproj):
    per_group = 6 if fuse_oproj else 3
    ctx_in, lat_in = refs[:per_group], refs[per_group:2 * per_group]
    refs = refs[2 * per_group:]
    if not fuse_oproj:
        refs = (None,) + refs
    *weights, outc_ref, outl_ref, hn_scr, x1_scr = refs
    shared = (*weights, hn_scr, x1_scr)
    i = pl.program_id(0)

    def run(group_in, out_ref, tile, tm, seq_len):
        o_refs = group_in[3:] if fuse_oproj else None
        _ffn_tile(group_in[:3], o_refs, out_ref, tile, shared, tm=tm, seq_len=seq_len)

    pl.when(i < n_ctx)(
        functools.partial(run, ctx_in, outc_ref, i, tm_ctx, t_ctx))
    pl.when(i >= n_ctx)(
        functools.partial(run, lat_in, outl_ref, i - n_ctx, tm_lat, t_lat))


def _ffn(xc, xl, mod, ng, w_up, conv_w, conv_b, w_down, *, layer, t_ctx, t_lat, ctx_row,
         oc=None, ol=None, w_o=None):
    d = xc.shape[1]
    fuse = oc is not None
    tm_ctx, tm_lat = min(512, t_ctx), min(512, t_lat)
    g = _TwoGroups(xc.shape[0] // tm_ctx, xl.shape[0] // tm_lat, tm_lat, t_lat, ctx_row)

    def group_specs(n_rows, tm, tile):
        hb = tm // HALO
        last = n_rows // HALO - 1
        return [pl.BlockSpec((tm, d), lambda i: (tile(i), 0)),
                pl.BlockSpec((HALO, d), lambda i: (jnp.maximum(tile(i) * hb - 1, 0), 0)),
                pl.BlockSpec((HALO, d), lambda i: (jnp.minimum((tile(i) + 1) * hb, last), 0))]

    args, specs = [], []
    for x, o, tm, tile in ((xc, oc, tm_ctx, g.ctx), (xl, ol, tm_lat, g.lat)):
        for a in ([x, o] if fuse else [x]):
            args += [a, a, a]
            specs += group_specs(x.shape[0], tm, tile)
    consts = ([w_o] if fuse else []) + [mod, ng, w_up, conv_w, conv_b, w_down]
    args += consts
    for a in consts:
        if a is mod:
            specs.append(g.mod_spec(d))
        else:
            specs.append(_const_spec(a.shape, layer if a is w_up or a is w_down else None))
    return pl.pallas_call(
        functools.partial(_ffn_kernel, n_ctx=g.n_ctx, tm_ctx=tm_ctx, t_ctx=t_ctx,
                          tm_lat=tm_lat, t_lat=t_lat, fuse_oproj=fuse),
        grid=(g.n_ctx + g.n_lat,),
        in_specs=specs,
        out_specs=[pl.BlockSpec((tm_ctx, d), lambda i: (g.ctx(i), 0)),
                   pl.BlockSpec((tm_lat, d), lambda i: (g.lat(i), 0))],
        out_shape=[jax.ShapeDtypeStruct(xc.shape, F32), jax.ShapeDtypeStruct(xl.shape, F32)],
        scratch_shapes=[pltpu.VMEM((max(tm_ctx, tm_lat) + HALO, d), BF16),
                        pltpu.VMEM((max(tm_ctx, tm_lat), d), F32)],
        compiler_params=_params("arbitrary"),
        name="ffn_oproj" if fuse else "ffn",
    )(*args)


def _store_keys(k_ref, kn, krb):
    for h in range(N_HEADS):
        lo = h * HEAD_PAD
        k_ref[:, lo:lo + NOPE_DIM] = kn[:, h * NOPE_DIM:(h + 1) * NOPE_DIM].astype(BF16)
        k_ref[:, lo + NOPE_DIM:lo + HEAD_PAD] = krb


def _store_values_t(vt_ref, wuvt_ref, ckvb):
    for lo, hi in _row_chunks(N_HEADS * V_DIM):
        vt_ref[0, lo:hi, :] = _dot_nt(wuvt_ref[lo:hi, :], ckvb).astype(BF16)


def _mla_proj_kernel(*refs, rope):
    (x_ref, mod_ref, ng_ref, wdown_ref, gkv_ref, gq_ref, wuq_ref,
     wuk_ref, wuvt_ref, *refs) = refs
    if rope:
        cos_ref, sin_ref, q_ref, k_ref, vt_ref = refs
        cos = cos_ref[...]
        sin = sin_ref[...]
    else:
        q_ref, k_ref, vt_ref, ckv_ref, kr_ref = refs
    m = mod_ref[0]
    ng = ng_ref[...]
    hb = (_rms(x_ref[...], ng[0:1]) * (1.0 + m[1:2]) + m[0:1]).astype(BF16)
    down = _mdot(hb, wdown_ref[...])
    ckv = _rms(down[:, :KV_RANK], gkv_ref[...])
    kr = down[:, KV_RANK:KV_RANK + ROPE_PAD]
    if rope:
        kr = kr * cos + pltpu.roll(kr, ROPE_DIM, axis=1) * sin
    else:
        ckv_ref[...] = ckv
        kr_ref[...] = kr[:, :ROPE_DIM]
    ql = _rms(down[:, KV_RANK + ROPE_PAD:], gq_ref[...]).astype(BF16)
    q = _mdot(ql, wuq_ref[...])
    ckvb = ckv.astype(BF16)
    _store_values_t(vt_ref, wuvt_ref, ckvb)
    _store_keys(k_ref, _mdot(ckvb, wuk_ref[...]), kr.astype(BF16))
    for h in range(N_HEADS):
        lo = h * HEAD_PAD
        mid = lo + NOPE_DIM
        q_ref[:, lo:mid] = (q[:, lo:mid] * QK_SCALE_LOG2E).astype(BF16)
        qr = q[:, mid:lo + HEAD_PAD]
        if rope:
            qr = qr * cos + pltpu.roll(qr, ROPE_DIM, axis=1) * sin
        q_ref[:, mid:lo + HEAD_PAD] = (qr * QK_SCALE_LOG2E).astype(BF16)


def _mla_proj(x, mod, ng, w, *, tm, seq_len, mod_row, rope_tables=None):
    n, d = x.shape
    rope = rope_tables is not None
    tiles_per_seq = seq_len // tm
    consts = [ng, w["down_rope"] if rope else w["down"], w["g_kv"], w["g_q"],
              w["uq_rope"] if rope else w["uq"], w["uk"], w["uv_t"]]
    args = [x, mod] + consts
    specs = [pl.BlockSpec((tm, d), lambda i: (i, 0)),
             pl.BlockSpec((1, 6, d), lambda i: (mod_row(i * tm), 0, 0))]
    specs += [_const_spec(a.shape) for a in consts]
    row = lambda i: (i, 0)
    col = lambda i: (i // tiles_per_seq, 0, i % tiles_per_seq)
    out_shape = [jax.ShapeDtypeStruct((n, N_HEADS * HEAD_PAD), BF16),
                 jax.ShapeDtypeStruct((n, N_HEADS * HEAD_PAD), BF16),
                 jax.ShapeDtypeStruct((n // seq_len, N_HEADS * V_DIM, seq_len), BF16)]
    out_specs = [pl.BlockSpec((tm, N_HEADS * HEAD_PAD), row),
                 pl.BlockSpec((tm, N_HEADS * HEAD_PAD), row),
                 pl.BlockSpec((1, N_HEADS * V_DIM, tm), col)]
    if rope:
        args += list(rope_tables)
        specs += [pl.BlockSpec((tm, ROPE_PAD), lambda i: (i % tiles_per_seq, 0))] * 2
    else:
        out_shape += [jax.ShapeDtypeStruct((n, KV_RANK), F32),
                      jax.ShapeDtypeStruct((n, ROPE_DIM), F32)]
        out_specs += [pl.BlockSpec((tm, KV_RANK), row),
                      pl.BlockSpec((tm, ROPE_DIM), row)]
    return pl.pallas_call(
        functools.partial(_mla_proj_kernel, rope=rope),
        grid=(n // tm,),
        in_specs=specs,
        out_specs=out_specs,
        out_shape=out_shape,
        compiler_params=_params("parallel"),
        name="mla_proj_rope" if rope else "mla_proj",
    )(*args)


def _kv_expand_kernel(ckv_ref, kr_ref, wuk_ref, wuvt_ref, k_ref, vt_ref):
    ckvb = ckv_ref[0].astype(BF16)
    _store_values_t(vt_ref, wuvt_ref, ckvb)
    _store_keys(k_ref.at[0], _mdot(ckvb, wuk_ref[...]), kr_ref[0].astype(BF16))


def _kv_expand(ckv, kr_pad, w_uk, w_uv_t):
    b, s, _ = ckv.shape
    blk = lambda i: (i, 0, 0)
    return pl.pallas_call(
        _kv_expand_kernel,
        grid=(b,),
        in_specs=[pl.BlockSpec((1, s, KV_RANK), blk),
                  pl.BlockSpec((1, s, ROPE_PAD), blk),
                  _const_spec(w_uk.shape), _const_spec(w_uv_t.shape)],
        out_specs=[pl.BlockSpec((1, s, N_HEADS * HEAD_PAD), blk),
                   pl.BlockSpec((1, N_HEADS * V_DIM, s), blk)],
        out_shape=[jax.ShapeDtypeStruct((b, s, N_HEADS * HEAD_PAD), BF16),
                   jax.ShapeDtypeStruct((b, N_HEADS * V_DIM, s), BF16)],
        compiler_params=_params("parallel"),
        name="kv_expand",
    )(ckv, kr_pad, w_uk, w_uv_t)


def _attn_kernel(*refs, n_kv, heads, t):
    q_ref = refs[0]
    kv_refs = refs[1:1 + 2 * n_kv]
    o_ref = refs[1 + 2 * n_kv]
    items = [(h, c) for h in range(heads) for c in range(t // Q_CHUNK)]

    def key_blocks(s):
        n_keys = kv_refs[2 * s].shape[1]
        step = min(KEY_CHUNK, n_keys)
        return [(r, r + step) for r in range(0, n_keys, step)]

    def scores(item):
        h, c = item
        q = q_ref[0, c * Q_CHUNK:(c + 1) * Q_CHUNK, h * HEAD_PAD:(h + 1) * HEAD_PAD]
        return [_dot_nt(kv_refs[2 * s][0, lo:hi, h * HEAD_PAD:(h + 1) * HEAD_PAD], q)
                for s in range(n_kv) for lo, hi in key_blocks(s)]

    def softmax_pv(item, blocks):
        h, c = item
        mx = functools.reduce(
            jnp.maximum, [jnp.max(b, axis=0, keepdims=True) for b in blocks])
        ps = [jnp.exp2(b - mx) for b in blocks]
        den = functools.reduce(
            jnp.add, [jnp.sum(p, axis=0, keepdims=True) for p in ps])
        acc = None
        first = 0
        for s in range(n_kv):
            nb = len(key_blocks(s))
            pt = jnp.concatenate([p.astype(BF16) for p in ps[first:first + nb]], axis=0)
            first += nb
            d = _dot(kv_refs[2 * s + 1][0, h * V_DIM:(h + 1) * V_DIM, :], pt)
            acc = d if acc is None else acc + d
        o = (acc * (1.0 / den)).T
        o_ref[0, c * Q_CHUNK:(c + 1) * Q_CHUNK, h * V_DIM:(h + 1) * V_DIM] = o.astype(BF16)

    nxt = scores(items[0])
    for n, item in enumerate(items):
        cur = nxt
        if n + 1 < len(items):
            nxt = scores(items[n + 1])
        softmax_pv(item, cur)


def _attention(q, kvs, *, heads):
    b, t, _ = q.shape
    args = [q]
    specs = [pl.BlockSpec((1, t, heads * HEAD_PAD), lambda bi, h: (bi, 0, h))]
    for k, vt in kvs:
        s = k.shape[1]
        args += [k, vt]
        specs += [pl.BlockSpec((1, s, heads * HEAD_PAD), lambda bi, h: (bi, 0, h)),
                  pl.BlockSpec((1, heads * V_DIM, s), lambda bi, h: (bi, h, 0))]
    return pl.pallas_call(
        functools.partial(_attn_kernel, n_kv=len(kvs), heads=heads, t=t),
        grid=(b, N_HEADS // heads),
        in_specs=specs,
        out_specs=pl.BlockSpec((1, t, heads * V_DIM), lambda bi, h: (bi, 0, h)),
        out_shape=jax.ShapeDtypeStruct((b, t, N_HEADS * V_DIM), BF16),
        compiler_params=_params("parallel", "parallel"),
        name="attention",
    )(*args)


_ROPE_SWAP = np.concatenate([np.arange(16, 32), np.arange(0, 16),
                             np.arange(48, 64), np.arange(32, 48)])


def _mla_weights(w_dq, g_q, w_uq, w_dkv, g_kv, w_uk, w_uv):
    zpad = jnp.zeros((D_MODEL, ROPE_PAD - ROPE_DIM), F32)
    k_rope = w_dkv[:, KV_RANK:]
    down = jnp.concatenate([w_dkv, zpad, w_dq], axis=1)
    down_rope = jnp.concatenate([w_dkv, k_rope[:, _ROPE_SWAP], w_dq], axis=1)
    uq = w_uq.reshape(Q_RANK, N_HEADS, NOPE_DIM + ROPE_DIM)
    q_rope = uq[:, :, NOPE_DIM:]
    hpad = jnp.zeros((Q_RANK, N_HEADS, ROPE_PAD - ROPE_DIM), F32)
    uq_full = jnp.concatenate([uq, hpad], axis=2).reshape(Q_RANK, N_HEADS * HEAD_PAD)
    uq_rope = jnp.concatenate([uq, q_rope[:, :, _ROPE_SWAP]], axis=2)
    uq_rope = uq_rope.reshape(Q_RANK, N_HEADS * HEAD_PAD)
    return dict(down=down.astype(BF16), down_rope=down_rope.astype(BF16),
                g_kv=g_kv.reshape(1, -1), g_q=g_q.reshape(1, -1),
                uq=uq_full.astype(BF16), uq_rope=uq_rope.astype(BF16),
                uk=w_uk.astype(BF16), uv_t=w_uv.T.astype(BF16))


def _rope_tables(t):
    half = ROPE_DIM // 4
    pos = np.arange(t)
    inv = ROPE_BASE ** (-np.arange(half, dtype=np.float64) / half)
    ang_r = (pos // GRID_W)[:, None] * inv[None, :]
    ang_c = (pos % GRID_W)[:, None] * inv[None, :]
    zeros = np.zeros((t, ROPE_PAD - ROPE_DIM))
    cos = np.concatenate([np.cos(ang_r), np.cos(ang_r), np.cos(ang_c),
                          np.cos(ang_c), zeros], axis=1)
    sin = np.concatenate([-np.sin(ang_r), np.sin(ang_r), -np.sin(ang_c),
                          np.sin(ang_c), zeros], axis=1)
    return jnp.asarray(cos, F32), jnp.asarray(sin, F32)


def kernel(x_prompt, x_sample, cache_ckv, cache_krope, c, c_ctx, ada_w, ada_b, norm_g, gm_w_in, gm_b_in, gm_g_v, gm_w_s, gm_b_s, gm_w_out, mla_w_dq, mla_g_q, mla_w_uq, mla_w_dkv, mla_g_kv, mla_w_uk, mla_w_uv, mla_w_o, ffn_w_up, ffn_conv_w, ffn_conv_b, ffn_w_down):
    nb_ctx, t_ctx, d = x_prompt.shape
    nb_lat, t_lat, _ = x_sample.shape
    ctx_row = nb_lat

    cond = jnp.zeros((N_COND, d), F32).at[:nb_lat].set(c).at[ctx_row].set(c_ctx)
    mod = _ada(cond, ada_w, ada_b).reshape(ada_w.shape[0], N_COND, 6, d)

    mixer_w = (gm_w_in[0].astype(BF16), gm_b_in[0].reshape(1, -1),
               gm_g_v[0].reshape(1, -1), gm_w_s[0].astype(BF16), gm_b_s[0].T,
               gm_w_out[0].astype(BF16))
    mla_w = _mla_weights(mla_w_dq[0], mla_g_q[0], mla_w_uq[0], mla_w_dkv[0],
                         mla_g_kv[0], mla_w_uk[0], mla_w_uv[0])

    groups = dict(t_ctx=t_ctx, t_lat=t_lat, ctx_row=ctx_row)
    xc = x_prompt.reshape(nb_ctx * t_ctx, d)
    xl = x_sample.reshape(nb_lat * t_lat, d)
    depth, _, up_cols = ffn_w_up.shape
    xc, xl, w_up, w_down, w_o = _mixer(
        xc, xl, mod[0], norm_g[0], *mixer_w, tm=512, t_lat=t_lat, ctx_row=ctx_row,
        casts=[ffn_w_up.reshape(depth * d, up_cols), ffn_w_down.reshape(depth * D_FF, d),
               mla_w_o[0]])
    w_up = w_up.reshape(depth, d, up_cols)
    w_down = w_down.reshape(depth, D_FF, d)
    ffn_w = [(w_up, ffn_conv_w[i], ffn_conv_b[i].reshape(1, -1), w_down) for i in range(depth)]
    xc, xl = _ffn(xc, xl, mod[0], norm_g[0], *ffn_w[0], layer=0, **groups)

    def attend(x, nb, t, mod_row, heads, rope_tables, cached):
        proj = _mla_proj(x, mod[1], norm_g[1], mla_w, tm=min(512, t), seq_len=t,
                         mod_row=mod_row, rope_tables=rope_tables)
        q, k = (a.reshape(nb, t, -1) for a in proj[:2])
        kvs = [(k, proj[2])] + ([cached] if cached is not None else [])
        return _attention(q, kvs, heads=heads).reshape(nb * t, -1), proj[3:]

    oc, (ckv, krope) = attend(xc, nb_ctx, t_ctx, lambda row: ctx_row, N_HEADS, None, None)
    kr_pad = jnp.pad(cache_krope[:, 0], ((0, 0), (0, 0), (0, ROPE_PAD - ROPE_DIM)))
    cached = _kv_expand(cache_ckv[:, 0], kr_pad, mla_w["uk"], mla_w["uv_t"])
    ol, _ = attend(xl, nb_lat, t_lat, lambda row: row // t_lat, 2, _rope_tables(t_lat),
                   cached)
    xc, xl = _ffn(xc, xl, mod[1], norm_g[1], *ffn_w[1], layer=1, **groups,
                  oc=oc, ol=ol, w_o=w_o)

    return (xc.reshape(nb_ctx, t_ctx, d), xl.reshape(nb_lat, t_lat, d),
            ckv.reshape(nb_ctx, 1, t_ctx, KV_RANK),
            krope.reshape(nb_ctx, 1, t_ctx, ROPE_DIM))
```

```python
import functools

import jax
import jax.numpy as jnp
import numpy as np
from jax import lax
from jax.experimental import pallas as pl
from jax.experimental.pallas import tpu as pltpu

D_MODEL = 1024
GRID_W = 64
CHUNK = 128
D_INNER = 2 * D_MODEL
N_GROUPS = 8
GROUP_DIM = D_INNER // N_GROUPS
N_HEADS = 8
NOPE_DIM = 128
ROPE_DIM = 64
V_DIM = 128
Q_RANK = 384
KV_RANK = 256
ROPE_BASE = 10000.0
D_FF = 2816
EPS = 1e-6

LANES = 128
BF16_SUBLANES = 16
MXU_DIM = 256
VMEM_LIMIT_BYTES = 56 * 1024 * 1024

HEAD_PAD = 2 * LANES
ROPE_PAD = LANES
N_COND = 16
HALO = BF16_SUBLANES
FF_TILE = MXU_DIM
MAX_DOT_ROWS = 176
CAST_STEPS = 32
Q_CHUNK = MXU_DIM
KEY_CHUNK = 2 * MXU_DIM
QK_SCALE_LOG2E = float((NOPE_DIM + ROPE_DIM) ** -0.5 * np.log2(np.e))

F32 = jnp.float32
BF16 = jnp.bfloat16


def _dot(a, b):
    return jnp.dot(a, b, preferred_element_type=F32)


def _dot_nt(a, b):
    return lax.dot_general(a, b, (((1,), (1,)), ((), ())),
                           preferred_element_type=F32)


def _row_chunks(rows):
    n = -(-rows // MAX_DOT_ROWS)
    step = -(-rows // (n * BF16_SUBLANES)) * BF16_SUBLANES
    return [(lo, min(lo + step, rows)) for lo in range(0, rows, step)]


def _mdot(a, w):
    return jnp.concatenate(
        [_dot(a[lo:hi], w) for lo, hi in _row_chunks(a.shape[0])], axis=0)


def _rms(x, g):
    return x * lax.rsqrt(jnp.mean(x * x, axis=-1, keepdims=True) + EPS) * g


def _sigmoid(x):
    return 1.0 / (1.0 + jnp.exp(-x))


def _gelu(x):
    return 0.5 * x * (1.0 + lax.erf(x * np.float32(np.sqrt(0.5))))


def _params(*sem):
    return pltpu.CompilerParams(dimension_semantics=sem,
                                vmem_limit_bytes=VMEM_LIMIT_BYTES)


def _const_spec(shape, layer=None):
    if layer is None:
        block, index = shape, (0,) * len(shape)
    else:
        block, index = (None,) + tuple(shape[1:]), (layer,) + (0,) * (len(shape) - 1)
    return pl.BlockSpec(block, lambda *_: index, pipeline_mode=pl.Buffered(1))


def _ada_kernel(cond_ref, w_ref, b_ref, o_ref):
    c = cond_ref[...]
    s = (c * _sigmoid(c)).astype(BF16)
    o_ref[0] = _dot(s, w_ref[0].astype(BF16)) + b_ref[0]


def _ada(cond, ada_w, ada_b):
    depth, d, n = ada_w.shape
    tn = 2048
    return pl.pallas_call(
        _ada_kernel,
        grid=(depth, n // tn),
        in_specs=[
            pl.BlockSpec((N_COND, d), lambda i, j: (0, 0)),
            pl.BlockSpec((1, d, tn), lambda i, j: (i, 0, j)),
            pl.BlockSpec((1, 1, tn), lambda i, j: (i, 0, j)),
        ],
        out_specs=pl.BlockSpec((1, N_COND, tn), lambda i, j: (i, 0, j)),
        out_shape=jax.ShapeDtypeStruct((depth, N_COND, n), F32),
        compiler_params=_params("parallel", "parallel"),
        name="ada",
    )(cond, ada_w, ada_b.reshape(depth, 1, n))


def _mixer_tile(x_ref, o_ref, mod_ref, ng_ref, w_in_ref, b_in_ref, gv_ref, ws_ref,
                bs_ref, w_out_ref, *, tm, casts):
    m = mod_ref[0]
    ng = ng_ref[...]
    x = x_ref[...]
    hb = (_rms(x, ng[0:1] * (1.0 + m[1:2])) + m[0:1]).astype(BF16)
    n_chunks = tm // CHUNK

    def pre(c):
        lo = c * GROUP_DIM
        return _mdot(hb, w_in_ref[:, lo:lo + GROUP_DIM]) + b_in_ref[:, lo:lo + GROUP_DIM]

    vs = []
    ss = None
    nxt = pre(N_GROUPS)
    for c in range(N_GROUPS):
        cur = nxt
        if c + 1 < N_GROUPS:
            nxt = pre(N_GROUPS + c + 1)
        v = _gelu(cur)
        s = jnp.sum(v * v, axis=-1, keepdims=True)
        ss = s if ss is None else ss + s
        vs.append(v)
    rs = lax.rsqrt(ss * (1.0 / D_INNER) + EPS)

    for src_ref, dst_ref in casts:
        dst_ref[...] = src_ref[...].astype(BF16)

    def mix(g):
        lo = g * GROUP_DIM
        vg = (vs[g] * rs * gv_ref[:, lo:lo + GROUP_DIM]).astype(BF16)
        w_s = ws_ref[g]
        return jnp.concatenate(
            [_dot(w_s, vg[n * CHUNK:(n + 1) * CHUNK]) for n in range(n_chunks)],
            axis=0)

    def down(t, g, acc):
        d = _mdot(t, w_out_ref[g * GROUP_DIM:(g + 1) * GROUP_DIM, :])
        return d if acc is None else acc + d

    acc = None
    t_prev = None
    nxt = (pre(0), mix(0))
    for g in range(N_GROUPS):
        u_pre, mixed = nxt
        if g + 1 < N_GROUPS:
            nxt = (pre(g + 1), mix(g + 1))
        if t_prev is not None:
            acc = down(t_prev, g - 1, acc)
        b_s = jnp.concatenate([bs_ref[:, g:g + 1]] * n_chunks, axis=0)
        t_prev = (_gelu(u_pre) * (mixed + b_s)).astype(BF16)
    acc = down(t_prev, N_GROUPS - 1, acc)
    o_ref[...] = x + _rms(acc, m[2:3] * ng[1:2])


def _mixer_kernel(xc_ref, xl_ref, *refs, tm, n_ctx, n_cast):
    ins, (oc_ref, ol_ref, *cast_out) = refs[:-(2 + n_cast)], refs[-(2 + n_cast):]
    shared, cast_in = ins[:len(ins) - n_cast], ins[len(ins) - n_cast:]
    tile = functools.partial(_mixer_tile, tm=tm, casts=tuple(zip(cast_in, cast_out)))
    i = pl.program_id(0)
    pl.when(i < n_ctx)(functools.partial(tile, xc_ref, oc_ref, *shared))
    pl.when(i >= n_ctx)(functools.partial(tile, xl_ref, ol_ref, *shared))


class _TwoGroups:
    def __init__(self, n_ctx, n_lat, tm_lat, t_lat, ctx_row):
        self.n_ctx, self.n_lat = n_ctx, n_lat
        self.tm_lat, self.t_lat, self.ctx_row = tm_lat, t_lat, ctx_row

    def ctx(self, i):
        return jnp.minimum(i, self.n_ctx - 1)

    def lat(self, i):
        return jnp.maximum(i - self.n_ctx, 0)

    def mod_spec(self, d):
        def index(i):
            lat_row = self.lat(i) * self.tm_lat // self.t_lat
            return (jnp.where(i < self.n_ctx, self.ctx_row, lat_row), 0, 0)
        return pl.BlockSpec((1, 6, d), index)


def _mixer(xc, xl, mod, ng, w_in, b_in, g_v, w_s, b_s_t, w_out, *, tm, t_lat, ctx_row,
           casts):
    d = xc.shape[1]
    g = _TwoGroups(xc.shape[0] // tm, xl.shape[0] // tm, tm, t_lat, ctx_row)
    n_steps = g.n_ctx + g.n_lat
    ctx_spec = pl.BlockSpec((tm, d), lambda i: (g.ctx(i), 0))
    lat_spec = pl.BlockSpec((tm, d), lambda i: (g.lat(i), 0))
    consts = [ng, w_in, b_in, g_v, w_s, b_s_t, w_out]

    def cast_spec(a):
        rows, cols = a.shape
        assert rows % (CAST_STEPS * BF16_SUBLANES) == 0 and CAST_STEPS <= n_steps
        return pl.BlockSpec((rows // CAST_STEPS, cols),
                            lambda i: (jnp.minimum(i, CAST_STEPS - 1), 0))

    cast_specs = [cast_spec(a) for a in casts]
    return pl.pallas_call(
        functools.partial(_mixer_kernel, tm=tm, n_ctx=g.n_ctx, n_cast=len(casts)),
        grid=(n_steps,),
        in_specs=([ctx_spec, lat_spec, g.mod_spec(d)]
                  + [_const_spec(a.shape) for a in consts] + cast_specs),
        out_specs=[ctx_spec, lat_spec] + cast_specs,
        out_shape=([jax.ShapeDtypeStruct(xc.shape, F32), jax.ShapeDtypeStruct(xl.shape, F32)]
                   + [jax.ShapeDtypeStruct(a.shape, BF16) for a in casts]),
        compiler_params=_params("arbitrary"),
        name="mixer",
    )(xc, xl, mod, *consts, *casts)


def _ffn_tile(x_refs, o_refs, out_ref, tile, shared, *, tm, seq_len):
    (wo_ref, mod_ref, ng_ref, wup_ref, cw_ref, cb_ref, wd_ref, hn_scr, x1_scr) = shared
    fuse_oproj = o_refs is not None
    x_ref, xp_ref, xn_ref = x_refs
    o_ref, op_ref, on_ref = o_refs if fuse_oproj else (None, None, None)
    m = mod_ref[0]
    ng = ng_ref[...]
    n_stages = D_FF // FF_TILE
    rows = tm + HALO

    def x1_of(xr, orf):
        if not fuse_oproj:
            return xr[...]
        return xr[...] + _rms(_dot(orf[...], wo_ref[...]), m[2:3] * ng[1:2])

    def hn_of(x1):
        return _rms(x1, ng[2:3] * (1.0 + m[4:5])) + m[3:4]

    x1 = x1_of(x_ref, o_ref)
    x1_scr[0:tm] = x1
    hn_scr[HALO:rows] = hn_of(x1).astype(BF16)
    row0 = tile * tm
    at_start = (row0 % seq_len) == 0
    at_end = ((row0 + tm) % seq_len) == 0
    hp = hn_of(x1_of(xp_ref, op_ref))
    hp = jnp.where(at_start, jnp.zeros_like(hp), hp)
    hx = hn_of(x1_of(xn_ref, on_ref))
    hx = jnp.where(at_end, jnp.zeros_like(hx), hx)
    rid = lax.broadcasted_iota(jnp.int32, hp.shape, 0)
    hn_scr[0:HALO] = jnp.where(rid == 0, hx, hp).astype(BF16)
    hn = hn_scr[0:rows]

    def up(j):
        lo = j * FF_TILE
        return (_mdot(hn, wup_ref[:, lo:lo + FF_TILE]),
                _mdot(hn, wup_ref[:, D_FF + lo:D_FF + lo + FF_TILE]))

    def conv(h, lo):
        cw = cw_ref[:, lo:lo + FF_TILE]
        return (pltpu.roll(h, 1, axis=0)[HALO:] * cw[0:1] + h[HALO:] * cw[1:2]
                + pltpu.roll(h, rows - 1, axis=0)[HALO:] * cw[2:3]
                + cb_ref[:, lo:lo + FF_TILE])

    def down(t, j, acc):
        d = _mdot(t, wd_ref[j * FF_TILE:(j + 1) * FF_TILE, :])
        return d if acc is None else acc + d

    nxt = up(0)
    acc = None
    t_prev = None
    for j in range(n_stages):
        hg, hv = nxt
        if j + 1 < n_stages:
            nxt = up(j + 1)
        if t_prev is not None:
            acc = down(t_prev, j - 1, acc)
        cg = conv(hg, j * FF_TILE)
        cv = conv(hv, D_FF + j * FF_TILE)
        t_prev = (cg * _sigmoid(cg) * cv).astype(BF16)
    acc = down(t_prev, n_stages - 1, acc)
    out_ref[...] = x1_scr[0:tm] + _rms(acc, m[5:6] * ng[3:4])


def _ffn_kernel(*refs, n_ctx, tm_ctx, t_ctx, tm_lat, t_lat, fuse_oproj):
    per_group = 6 if fuse_oproj else 3
    ctx_in, lat_in = refs[:per_group], refs[per_group:2 * per_group]
    refs = refs[2 * per_group:]
    if not fuse_oproj:
        refs = (None,) + refs
    *weights, outc_ref, outl_ref, hn_scr, x1_scr = refs
    shared = (*weights, hn_scr, x1_scr)
    i = pl.program_id(0)

    def run(group_in, out_ref, tile, tm, seq_len):
        o_refs = group_in[3:] if fuse_oproj else None
        _ffn_tile(group_in[:3], o_refs, out_ref, tile, shared, tm=tm, seq_len=seq_len)

    pl.when(i < n_ctx)(
        functools.partial(run, ctx_in, outc_ref, i, tm_ctx, t_ctx))
    pl.when(i >= n_ctx)(
        functools.partial(run, lat_in, outl_ref, i - n_ctx, tm_lat, t_lat))


def _ffn(xc, xl, mod, ng, w_up, conv_w, conv_b, w_down, *, layer, t_ctx, t_lat, ctx_row,
         oc=None, ol=None, w_o=None):
    d = xc.shape[1]
    fuse = oc is not None
    tm_ctx, tm_lat = min(512, t_ctx), min(512, t_lat)
    g = _TwoGroups(xc.shape[0] // tm_ctx, xl.shape[0] // tm_lat, tm_lat, t_lat, ctx_row)

    def group_specs(n_rows, tm, tile):
        hb = tm // HALO
        last = n_rows // HALO - 1
        return [pl.BlockSpec((tm, d), lambda i: (tile(i), 0)),
                pl.BlockSpec((HALO, d), lambda i: (jnp.maximum(tile(i) * hb - 1, 0), 0)),
                pl.BlockSpec((HALO, d), lambda i: (jnp.minimum((tile(i) + 1) * hb, last), 0))]

    args, specs = [], []
    for x, o, tm, tile in ((xc, oc, tm_ctx, g.ctx), (xl, ol, tm_lat, g.lat)):
        for a in ([x, o] if fuse else [x]):
            args += [a, a, a]
            specs += group_specs(x.shape[0], tm, tile)
    consts = ([w_o] if fuse else []) + [mod, ng, w_up, conv_w, conv_b, w_down]
    args += consts
    for a in consts:
        if a is mod:
            specs.append(g.mod_spec(d))
        else:
            specs.append(_const_spec(a.shape, layer if a is w_up or a is w_down else None))
    return pl.pallas_call(
        functools.partial(_ffn_kernel, n_ctx=g.n_ctx, tm_ctx=tm_ctx, t_ctx=t_ctx,
                          tm_lat=tm_lat, t_lat=t_lat, fuse_oproj=fuse),
        grid=(g.n_ctx + g.n_lat,),
        in_specs=specs,
        out_specs=[pl.BlockSpec((tm_ctx, d), lambda i: (g.ctx(i), 0)),
                   pl.BlockSpec((tm_lat, d), lambda i: (g.lat(i), 0))],
        out_shape=[jax.ShapeDtypeStruct(xc.shape, F32), jax.ShapeDtypeStruct(xl.shape, F32)],
        scratch_shapes=[pltpu.VMEM((max(tm_ctx, tm_lat) + HALO, d), BF16),
                        pltpu.VMEM((max(tm_ctx, tm_lat), d), F32)],
        compiler_params=_params("arbitrary"),
        name="ffn_oproj" if fuse else "ffn",
    )(*args)


def _store_keys(k_ref, kn, krb):
    for h in range(N_HEADS):
        lo = h * HEAD_PAD
        k_ref[:, lo:lo + NOPE_DIM] = kn[:, h * NOPE_DIM:(h + 1) * NOPE_DIM].astype(BF16)
        k_ref[:, lo + NOPE_DIM:lo + HEAD_PAD] = krb


def _store_values_t(vt_ref, wuvt_ref, ckvb):
    for lo, hi in _row_chunks(N_HEADS * V_DIM):
        vt_ref[0, lo:hi, :] = _dot_nt(wuvt_ref[lo:hi, :], ckvb).astype(BF16)


def _mla_proj_kernel(*refs, rope):
    (x_ref, mod_ref, ng_ref, wdown_ref, gkv_ref, gq_ref, wuq_ref,
     wuk_ref, wuvt_ref, *refs) = refs
    if rope:
        cos_ref, sin_ref, q_ref, k_ref, vt_ref = refs
        cos = cos_ref[...]
        sin = sin_ref[...]
    else:
        q_ref, k_ref, vt_ref, ckv_ref, kr_ref = refs
    m = mod_ref[0]
    ng = ng_ref[...]
    hb = (_rms(x_ref[...], ng[0:1] * (1.0 + m[1:2])) + m[0:1]).astype(BF16)
    down = _mdot(hb, wdown_ref[...])
    ckv = _rms(down[:, :KV_RANK], gkv_ref[...])
    kr = down[:, KV_RANK:KV_RANK + ROPE_PAD]
    if rope:
        kr = kr * cos + pltpu.roll(kr, ROPE_DIM, axis=1) * sin
    else:
        ckv_ref[...] = ckv
        kr_ref[...] = kr[:, :ROPE_DIM]
    ql = _rms(down[:, KV_RANK + ROPE_PAD:], gq_ref[...]).astype(BF16)
    q = _mdot(ql, wuq_ref[...])
    ckvb = ckv.astype(BF16)
    _store_values_t(vt_ref, wuvt_ref, ckvb)
    _store_keys(k_ref, _mdot(ckvb, wuk_ref[...]), kr.astype(BF16))
    for h in range(N_HEADS):
        lo = h * HEAD_PAD
        mid = lo + NOPE_DIM
        q_ref[:, lo:mid] = (q[:, lo:mid] * QK_SCALE_LOG2E).astype(BF16)
        qr = q[:, mid:lo + HEAD_PAD]
        if rope:
            qr = qr * cos + pltpu.roll(qr, ROPE_DIM, axis=1) * sin
        q_ref[:, mid:lo + HEAD_PAD] = (qr * QK_SCALE_LOG2E).astype(BF16)


def _mla_proj(x, mod, ng, w, *, tm, seq_len, mod_row, rope_tables=None):
    n, d = x.shape
    rope = rope_tables is not None
    tiles_per_seq = seq_len // tm
    consts = [ng, w["down_rope"] if rope else w["down"], w["g_kv"], w["g_q"],
              w["uq_rope"] if rope else w["uq"], w["uk"], w["uv_t"]]
    args = [x, mod] + consts
    specs = [pl.BlockSpec((tm, d), lambda i: (i, 0)),
             pl.BlockSpec((1, 6, d), lambda i: (mod_row(i * tm), 0, 0))]
    specs += [_const_spec(a.shape) for a in consts]
    row = lambda i: (i, 0)
    col = lambda i: (i // tiles_per_seq, 0, i % tiles_per_seq)
    out_shape = [jax.ShapeDtypeStruct((n, N_HEADS * HEAD_PAD), BF16),
                 jax.ShapeDtypeStruct((n, N_HEADS * HEAD_PAD), BF16),
                 jax.ShapeDtypeStruct((n // seq_len, N_HEADS * V_DIM, seq_len), BF16)]
    out_specs = [pl.BlockSpec((tm, N_HEADS * HEAD_PAD), row),
                 pl.BlockSpec((tm, N_HEADS * HEAD_PAD), row),
                 pl.BlockSpec((1, N_HEADS * V_DIM, tm), col)]
    if rope:
        args += list(rope_tables)
        specs += [pl.BlockSpec((tm, ROPE_PAD), lambda i: (i % tiles_per_seq, 0))] * 2
    else:
        out_shape += [jax.ShapeDtypeStruct((n, KV_RANK), F32),
                      jax.ShapeDtypeStruct((n, ROPE_DIM), F32)]
        out_specs += [pl.BlockSpec((tm, KV_RANK), row),
                      pl.BlockSpec((tm, ROPE_DIM), row)]
    return pl.pallas_call(
        functools.partial(_mla_proj_kernel, rope=rope),
        grid=(n // tm,),
        in_specs=specs,
        out_specs=out_specs,
        out_shape=out_shape,
        compiler_params=_params("parallel"),
        name="mla_proj_rope" if rope else "mla_proj",
    )(*args)


def _kv_expand_kernel(ckv_ref, kr_ref, wuk_ref, wuvt_ref, k_ref, vt_ref):
    ckvb = ckv_ref[0].astype(BF16)
    _store_values_t(vt_ref, wuvt_ref, ckvb)
    _store_keys(k_ref.at[0], _mdot(ckvb, wuk_ref[...]), kr_ref[0].astype(BF16))


def _kv_expand(ckv, kr_pad, w_uk, w_uv_t):
    b, s, _ = ckv.shape
    blk = lambda i: (i, 0, 0)
    return pl.pallas_call(
        _kv_expand_kernel,
        grid=(b,),
        in_specs=[pl.BlockSpec((1, s, KV_RANK), blk),
                  pl.BlockSpec((1, s, ROPE_PAD), blk),
                  _const_spec(w_uk.shape), _const_spec(w_uv_t.shape)],
        out_specs=[pl.BlockSpec((1, s, N_HEADS * HEAD_PAD), blk),
                   pl.BlockSpec((1, N_HEADS * V_DIM, s), blk)],
        out_shape=[jax.ShapeDtypeStruct((b, s, N_HEADS * HEAD_PAD), BF16),
                   jax.ShapeDtypeStruct((b, N_HEADS * V_DIM, s), BF16)],
        compiler_params=_params("parallel"),
        name="kv_expand",
    )(ckv, kr_pad, w_uk, w_uv_t)


def _attn_kernel(*refs, n_kv, heads, t):
    q_ref = refs[0]
    kv_refs = refs[1:1 + 2 * n_kv]
    o_ref = refs[1 + 2 * n_kv]
    items = [(h, c) for h in range(heads) for c in range(t // Q_CHUNK)]

    def key_blocks(s):
        n_keys = kv_refs[2 * s].shape[1]
        step = min(KEY_CHUNK, n_keys)
        return [(r, r + step) for r in range(0, n_keys, step)]

    def scores(item):
        h, c = item
        q = q_ref[0, c * Q_CHUNK:(c + 1) * Q_CHUNK, h * HEAD_PAD:(h + 1) * HEAD_PAD]
        return [_dot_nt(kv_refs[2 * s][0, lo:hi, h * HEAD_PAD:(h + 1) * HEAD_PAD], q)
                for s in range(n_kv) for lo, hi in key_blocks(s)]

    def softmax_pv(item, blocks):
        h, c = item
        mx = functools.reduce(
            jnp.maximum, [jnp.max(b, axis=0, keepdims=True) for b in blocks])
        ps = [jnp.exp2(b - mx) for b in blocks]
        den = functools.reduce(
            jnp.add, [jnp.sum(p, axis=0, keepdims=True) for p in ps])
        acc = None
        first = 0
        for s in range(n_kv):
            nb = len(key_blocks(s))
            pt = jnp.concatenate([p.astype(BF16) for p in ps[first:first + nb]], axis=0)
            first += nb
            d = _dot(kv_refs[2 * s + 1][0, h * V_DIM:(h + 1) * V_DIM, :], pt)
            acc = d if acc is None else acc + d
        o = (acc * (1.0 / den)).T
        o_ref[0, c * Q_CHUNK:(c + 1) * Q_CHUNK, h * V_DIM:(h + 1) * V_DIM] = o.astype(BF16)

    nxt = scores(items[0])
    for n, item in enumerate(items):
        cur = nxt
        if n + 1 < len(items):
            nxt = scores(items[n + 1])
        softmax_pv(item, cur)


def _attention(q, kvs, *, heads):
    b, t, _ = q.shape
    args = [q]
    specs = [pl.BlockSpec((1, t, heads * HEAD_PAD), lambda bi, h: (bi, 0, h))]
    for k, vt in kvs:
        s = k.shape[1]
        args += [k, vt]
        specs += [pl.BlockSpec((1, s, heads * HEAD_PAD), lambda bi, h: (bi, 0, h)),
                  pl.BlockSpec((1, heads * V_DIM, s), lambda bi, h: (bi, h, 0))]
    return pl.pallas_call(
        functools.partial(_attn_kernel, n_kv=len(kvs), heads=heads, t=t),
        grid=(b, N_HEADS // heads),
        in_specs=specs,
        out_specs=pl.BlockSpec((1, t, heads * V_DIM), lambda bi, h: (bi, 0, h)),
        out_shape=jax.ShapeDtypeStruct((b, t, N_HEADS * V_DIM), BF16),
        compiler_params=_params("parallel", "parallel"),
        name="attention",
    )(*args)


_ROPE_SWAP = np.concatenate([np.arange(16, 32), np.arange(0, 16),
                             np.arange(48, 64), np.arange(32, 48)])


def _mla_weights(w_dq, g_q, w_uq, w_dkv, g_kv, w_uk, w_uv):
    zpad = jnp.zeros((D_MODEL, ROPE_PAD - ROPE_DIM), F32)
    k_rope = w_dkv[:, KV_RANK:]
    down = jnp.concatenate([w_dkv, zpad, w_dq], axis=1)
    down_rope = jnp.concatenate([w_dkv, k_rope[:, _ROPE_SWAP], w_dq], axis=1)
    uq = w_uq.reshape(Q_RANK, N_HEADS, NOPE_DIM + ROPE_DIM)
    q_rope = uq[:, :, NOPE_DIM:]
    hpad = jnp.zeros((Q_RANK, N_HEADS, ROPE_PAD - ROPE_DIM), F32)
    uq_full = jnp.concatenate([uq, hpad], axis=2).reshape(Q_RANK, N_HEADS * HEAD_PAD)
    uq_rope = jnp.concatenate([uq, q_rope[:, :, _ROPE_SWAP]], axis=2)
    uq_rope = uq_rope.reshape(Q_RANK, N_HEADS * HEAD_PAD)
    return dict(down=down.astype(BF16), down_rope=down_rope.astype(BF16),
                g_kv=g_kv.reshape(1, -1), g_q=g_q.reshape(1, -1),
                uq=uq_full.astype(BF16), uq_rope=uq_rope.astype(BF16),
                uk=w_uk.astype(BF16), uv_t=w_uv.T.astype(BF16))


def _rope_tables(t):
    half = ROPE_DIM // 4
    pos = np.arange(t)
    inv = ROPE_BASE ** (-np.arange(half, dtype=np.float64) / half)
    ang_r = (pos // GRID_W)[:, None] * inv[None, :]
    ang_c = (pos % GRID_W)[:, None] * inv[None, :]
    zeros = np.zeros((t, ROPE_PAD - ROPE_DIM))
    cos = np.concatenate([np.cos(ang_r), np.cos(ang_r), np.cos(ang_c),
                          np.cos(ang_c), zeros], axis=1)
    sin = np.concatenate([-np.sin(ang_r), np.sin(ang_r), -np.sin(ang_c),
                          np.sin(ang_c), zeros], axis=1)
    return jnp.asarray(cos, F32), jnp.asarray(sin, F32)


def kernel(x_prompt, x_sample, cache_ckv, cache_krope, c, c_ctx, ada_w, ada_b, norm_g, gm_w_in, gm_b_in, gm_g_v, gm_w_s, gm_b_s, gm_w_out, mla_w_dq, mla_g_q, mla_w_uq, mla_w_dkv, mla_g_kv, mla_w_uk, mla_w_uv, mla_w_o, ffn_w_up, ffn_conv_w, ffn_conv_b, ffn_w_down):
    nb_ctx, t_ctx, d = x_prompt.shape
    nb_lat, t_lat, _ = x_sample.shape
    ctx_row = nb_lat

    cond = jnp.concatenate(
        [c, c_ctx[None, :], jnp.zeros((N_COND - nb_lat - 1, d), F32)], axis=0)
    mod = _ada(cond, ada_w, ada_b).reshape(ada_w.shape[0], N_COND, 6, d)

    mixer_w = (gm_w_in[0].astype(BF16), gm_b_in[0].reshape(1, -1),
               gm_g_v[0].reshape(1, -1), gm_w_s[0].astype(BF16), gm_b_s[0].T,
               gm_w_out[0].astype(BF16))
    mla_w = _mla_weights(mla_w_dq[0], mla_g_q[0], mla_w_uq[0], mla_w_dkv[0],
                         mla_g_kv[0], mla_w_uk[0], mla_w_uv[0])

    groups = dict(t_ctx=t_ctx, t_lat=t_lat, ctx_row=ctx_row)
    xc = x_prompt.reshape(nb_ctx * t_ctx, d)
    xl = x_sample.reshape(nb_lat * t_lat, d)
    depth, _, up_cols = ffn_w_up.shape
    xc, xl, w_up, w_down, w_o = _mixer(
        xc, xl, mod[0], norm_g[0], *mixer_w, tm=512, t_lat=t_lat, ctx_row=ctx_row,
        casts=[ffn_w_up.reshape(depth * d, up_cols), ffn_w_down.reshape(depth * D_FF, d),
               mla_w_o[0]])
    w_up = w_up.reshape(depth, d, up_cols)
    w_down = w_down.reshape(depth, D_FF, d)
    ffn_w = [(w_up, ffn_conv_w[i], ffn_conv_b[i].reshape(1, -1), w_down) for i in range(depth)]
    xc, xl = _ffn(xc, xl, mod[0], norm_g[0], *ffn_w[0], layer=0, **groups)

    def attend(x, nb, t, mod_row, heads, rope_tables, cached):
        proj = _mla_proj(x, mod[1], norm_g[1], mla_w, tm=min(512, t), seq_len=t,
                         mod_row=mod_row, rope_tables=rope_tables)
        q, k = (a.reshape(nb, t, -1) for a in proj[:2])
        kvs = [(k, proj[2])] + ([cached] if cached is not None else [])
        return _attention(q, kvs, heads=heads).reshape(nb * t, -1), proj[3:]

    oc, (ckv, krope) = attend(xc, nb_ctx, t_ctx, lambda row: ctx_row, N_HEADS, None, None)
    kr_pad = jnp.pad(cache_krope[:, 0], ((0, 0), (0, 0), (0, ROPE_PAD - ROPE_DIM)))
    cached = _kv_expand(cache_ckv[:, 0], kr_pad, mla_w["uk"], mla_w["uv_t"])
    ol, _ = attend(xl, nb_lat, t_lat, lambda row: row // t_lat, 2, _rope_tables(t_lat),
                   cached)
    xc, xl = _ffn(xc, xl, mod[1], norm_g[1], *ffn_w[1], layer=1, **groups,
                  oc=oc, ol=ol, w_o=w_o)

    return (xc.reshape(nb_ctx, t_ctx, d), xl.reshape(nb_lat, t_lat, d),
            ckv.reshape(nb_ctx, 1, t_ctx, KV_RANK),
            krope.reshape(nb_ctx, 1, t_ctx, ROPE_DIM))
```

```python
import functools

import jax
import jax.numpy as jnp
import numpy as np
from jax import lax
from jax.experimental import pallas as pl
from jax.experimental.pallas import tpu as pltpu

D_MODEL = 1024
GRID_W = 64
CHUNK = 128
D_INNER = 2 * D_MODEL
N_GROUPS = 8
GROUP_DIM = D_INNER // N_GROUPS
N_HEADS = 8
NOPE_DIM = 128
ROPE_DIM = 64
V_DIM = 128
Q_RANK = 384
KV_RANK = 256
ROPE_BASE = 10000.0
D_FF = 2816
EPS = 1e-6

LANES = 128
BF16_SUBLANES = 16
MXU_DIM = 256
VMEM_LIMIT_BYTES = 56 * 1024 * 1024

HEAD_PAD = 2 * LANES
ROPE_PAD = LANES
N_COND = 16
HALO = BF16_SUBLANES
FF_TILE = MXU_DIM
TOKEN_TILE = 512
ADA_COL_TILE = 2048
LATENT_HEADS_PER_STEP = 2
MAX_DOT_ROWS = 176
CAST_STEPS = 32
Q_CHUNK = MXU_DIM
KEY_CHUNK = 2 * MXU_DIM
QK_SCALE_LOG2E = float((NOPE_DIM + ROPE_DIM) ** -0.5 * np.log2(np.e))

F32 = jnp.float32
BF16 = jnp.bfloat16


def _dot(a, b):
    return jnp.dot(a, b, preferred_element_type=F32)


def _dot_nt(a, b):
    return lax.dot_general(a, b, (((1,), (1,)), ((), ())),
                           preferred_element_type=F32)


def _row_chunks(rows):
    n = -(-rows // MAX_DOT_ROWS)
    step = -(-rows // (n * BF16_SUBLANES)) * BF16_SUBLANES
    return [(lo, min(lo + step, rows)) for lo in range(0, rows, step)]


def _mdot(a, w):
    return jnp.concatenate(
        [_dot(a[lo:hi], w) for lo, hi in _row_chunks(a.shape[0])], axis=0)


def _rms(x, g):
    return x * lax.rsqrt(jnp.mean(x * x, axis=-1, keepdims=True) + EPS) * g


def _sigmoid(x):
    return 1.0 / (1.0 + jnp.exp(-x))


def _gelu(x):
    return 0.5 * x * (1.0 + lax.erf(x * np.float32(np.sqrt(0.5))))


def _params(*sem):
    return pltpu.CompilerParams(dimension_semantics=sem,
                                vmem_limit_bytes=VMEM_LIMIT_BYTES)


def _const_spec(shape, layer=None):
    if layer is None:
        block, index = shape, (0,) * len(shape)
    else:
        block, index = (None,) + tuple(shape[1:]), (layer,) + (0,) * (len(shape) - 1)
    return pl.BlockSpec(block, lambda *_: index, pipeline_mode=pl.Buffered(1))


def _ada_kernel(cond_ref, w_ref, b_ref, o_ref):
    c = cond_ref[...]
    s = (c * _sigmoid(c)).astype(BF16)
    o_ref[0] = _dot(s, w_ref[0].astype(BF16)) + b_ref[0]


def _ada(cond, ada_w, ada_b):
    depth, d, n = ada_w.shape
    tn = ADA_COL_TILE
    return pl.pallas_call(
        _ada_kernel,
        grid=(depth, n // tn),
        in_specs=[
            pl.BlockSpec((N_COND, d), lambda i, j: (0, 0)),
            pl.BlockSpec((1, d, tn), lambda i, j: (i, 0, j)),
            pl.BlockSpec((1, 1, tn), lambda i, j: (i, 0, j)),
        ],
        out_specs=pl.BlockSpec((1, N_COND, tn), lambda i, j: (i, 0, j)),
        out_shape=jax.ShapeDtypeStruct((depth, N_COND, n), F32),
        compiler_params=_params("parallel", "parallel"),
        name="ada",
    )(cond, ada_w, ada_b.reshape(depth, 1, n))


def _mixer_tile(x_ref, o_ref, mod_ref, ng_ref, w_in_ref, b_in_ref, gv_ref, ws_ref,
                bs_ref, w_out_ref, *, tm, casts):
    m = mod_ref[0]
    ng = ng_ref[...]
    x = x_ref[...]
    hb = (_rms(x, ng[0:1] * (1.0 + m[1:2])) + m[0:1]).astype(BF16)
    n_chunks = tm // CHUNK

    def pre(c):
        lo = c * GROUP_DIM
        return _mdot(hb, w_in_ref[:, lo:lo + GROUP_DIM]) + b_in_ref[:, lo:lo + GROUP_DIM]

    vs = []
    ss = None
    nxt = pre(N_GROUPS)
    for c in range(N_GROUPS):
        cur = nxt
        if c + 1 < N_GROUPS:
            nxt = pre(N_GROUPS + c + 1)
        v = _gelu(cur)
        s = jnp.sum(v * v, axis=-1, keepdims=True)
        ss = s if ss is None else ss + s
        vs.append(v)
    rs = lax.rsqrt(ss * (1.0 / D_INNER) + EPS)

    for src_ref, dst_ref in casts:
        dst_ref[...] = src_ref[...].astype(BF16)

    def mix(g):
        lo = g * GROUP_DIM
        vg = (vs[g] * rs * gv_ref[:, lo:lo + GROUP_DIM]).astype(BF16)
        w_s = ws_ref[g]
        return jnp.concatenate(
            [_dot(w_s, vg[n * CHUNK:(n + 1) * CHUNK]) for n in range(n_chunks)],
            axis=0)

    def down(t, g, acc):
        d = _mdot(t, w_out_ref[g * GROUP_DIM:(g + 1) * GROUP_DIM, :])
        return d if acc is None else acc + d

    acc = None
    t_prev = None
    nxt = (pre(0), mix(0))
    for g in range(N_GROUPS):
        u_pre, mixed = nxt
        if g + 1 < N_GROUPS:
            nxt = (pre(g + 1), mix(g + 1))
        if t_prev is not None:
            acc = down(t_prev, g - 1, acc)
        b_s = jnp.concatenate([bs_ref[:, g:g + 1]] * n_chunks, axis=0)
        t_prev = (_gelu(u_pre) * (mixed + b_s)).astype(BF16)
    acc = down(t_prev, N_GROUPS - 1, acc)
    o_ref[...] = x + _rms(acc, m[2:3] * ng[1:2])


def _mixer_kernel(xc_ref, xl_ref, *refs, tm, n_ctx, n_cast):
    ins, (oc_ref, ol_ref, *cast_out) = refs[:-(2 + n_cast)], refs[-(2 + n_cast):]
    shared, cast_in = ins[:len(ins) - n_cast], ins[len(ins) - n_cast:]
    tile = functools.partial(_mixer_tile, tm=tm, casts=tuple(zip(cast_in, cast_out)))
    i = pl.program_id(0)
    pl.when(i < n_ctx)(functools.partial(tile, xc_ref, oc_ref, *shared))
    pl.when(i >= n_ctx)(functools.partial(tile, xl_ref, ol_ref, *shared))


class _TwoGroups:
    def __init__(self, n_ctx, n_lat, tm_lat, t_lat, ctx_row):
        self.n_ctx, self.n_lat = n_ctx, n_lat
        self.tm_lat, self.t_lat, self.ctx_row = tm_lat, t_lat, ctx_row

    def ctx(self, i):
        return jnp.minimum(i, self.n_ctx - 1)

    def lat(self, i):
        return jnp.maximum(i - self.n_ctx, 0)

    def mod_spec(self, d):
        def index(i):
            lat_row = self.lat(i) * self.tm_lat // self.t_lat
            return (jnp.where(i < self.n_ctx, self.ctx_row, lat_row), 0, 0)
        return pl.BlockSpec((1, 6, d), index)


def _mixer(xc, xl, mod, ng, w_in, b_in, g_v, w_s, b_s_t, w_out, *, tm, t_lat, ctx_row,
           casts):
    d = xc.shape[1]
    g = _TwoGroups(xc.shape[0] // tm, xl.shape[0] // tm, tm, t_lat, ctx_row)
    n_steps = g.n_ctx + g.n_lat
    ctx_spec = pl.BlockSpec((tm, d), lambda i: (g.ctx(i), 0))
    lat_spec = pl.BlockSpec((tm, d), lambda i: (g.lat(i), 0))
    consts = [ng, w_in, b_in, g_v, w_s, b_s_t, w_out]

    def cast_spec(a):
        rows, cols = a.shape
        assert rows % (CAST_STEPS * BF16_SUBLANES) == 0 and CAST_STEPS <= n_steps
        return pl.BlockSpec((rows // CAST_STEPS, cols),
                            lambda i: (jnp.minimum(i, CAST_STEPS - 1), 0))

    cast_specs = [cast_spec(a) for a in casts]
    return pl.pallas_call(
        functools.partial(_mixer_kernel, tm=tm, n_ctx=g.n_ctx, n_cast=len(casts)),
        grid=(n_steps,),
        in_specs=([ctx_spec, lat_spec, g.mod_spec(d)]
                  + [_const_spec(a.shape) for a in consts] + cast_specs),
        out_specs=[ctx_spec, lat_spec] + cast_specs,
        out_shape=([jax.ShapeDtypeStruct(xc.shape, F32), jax.ShapeDtypeStruct(xl.shape, F32)]
                   + [jax.ShapeDtypeStruct(a.shape, BF16) for a in casts]),
        compiler_params=_params("arbitrary"),
        name="mixer",
    )(xc, xl, mod, *consts, *casts)


def _ffn_tile(x_refs, o_refs, out_ref, tile, shared, *, tm, seq_len):
    (wo_ref, mod_ref, ng_ref, wup_ref, cw_ref, cb_ref, wd_ref, hn_scr, x1_scr) = shared
    fuse_oproj = o_refs is not None
    x_ref, xp_ref, xn_ref = x_refs
    o_ref, op_ref, on_ref = o_refs if fuse_oproj else (None, None, None)
    m = mod_ref[0]
    ng = ng_ref[...]
    n_stages = D_FF // FF_TILE
    rows = tm + HALO

    def x1_of(xr, orf):
        if not fuse_oproj:
            return xr[...]
        return xr[...] + _rms(_dot(orf[...], wo_ref[...]), m[2:3] * ng[1:2])

    def hn_of(x1):
        return _rms(x1, ng[2:3] * (1.0 + m[4:5])) + m[3:4]

    x1 = x1_of(x_ref, o_ref)
    x1_scr[0:tm] = x1
    hn_scr[HALO:rows] = hn_of(x1).astype(BF16)
    row0 = tile * tm
    at_start = (row0 % seq_len) == 0
    at_end = ((row0 + tm) % seq_len) == 0
    hp = hn_of(x1_of(xp_ref, op_ref))
    hp = jnp.where(at_start, jnp.zeros_like(hp), hp)
    hx = hn_of(x1_of(xn_ref, on_ref))
    hx = jnp.where(at_end, jnp.zeros_like(hx), hx)
    rid = lax.broadcasted_iota(jnp.int32, hp.shape, 0)
    hn_scr[0:HALO] = jnp.where(rid == 0, hx, hp).astype(BF16)
    hn = hn_scr[0:rows]

    def up(j):
        lo = j * FF_TILE
        return (_mdot(hn, wup_ref[:, lo:lo + FF_TILE]),
                _mdot(hn, wup_ref[:, D_FF + lo:D_FF + lo + FF_TILE]))

    def conv(h, lo):
        cw = cw_ref[:, lo:lo + FF_TILE]
        return (pltpu.roll(h, 1, axis=0)[HALO:] * cw[0:1] + h[HALO:] * cw[1:2]
                + pltpu.roll(h, rows - 1, axis=0)[HALO:] * cw[2:3]
                + cb_ref[:, lo:lo + FF_TILE])

    def down(t, j, acc):
        d = _mdot(t, wd_ref[j * FF_TILE:(j + 1) * FF_TILE, :])
        return d if acc is None else acc + d

    nxt = up(0)
    acc = None
    t_prev = None
    for j in range(n_stages):
        hg, hv = nxt
        if j + 1 < n_stages:
            nxt = up(j + 1)
        if t_prev is not None:
            acc = down(t_prev, j - 1, acc)
        cg = conv(hg, j * FF_TILE)
        cv = conv(hv, D_FF + j * FF_TILE)
        t_prev = (cg * _sigmoid(cg) * cv).astype(BF16)
    acc = down(t_prev, n_stages - 1, acc)
    out_ref[...] = x1_scr[0:tm] + _rms(acc, m[5:6] * ng[3:4])


def _ffn_kernel(*refs, n_ctx, tm_ctx, t_ctx, tm_lat, t_lat, fuse_oproj):
    per_group = 6 if fuse_oproj else 3
    ctx_in, lat_in = refs[:per_group], refs[per_group:2 * per_group]
    refs = refs[2 * per_group:]
    if not fuse_oproj:
        refs = (None,) + refs
    *weights, outc_ref, outl_ref, hn_scr, x1_scr = refs
    shared = (*weights, hn_scr, x1_scr)
    i = pl.program_id(0)

    def run(group_in, out_ref, tile, tm, seq_len):
        o_refs = group_in[3:] if fuse_oproj else None
        _ffn_tile(group_in[:3], o_refs, out_ref, tile, shared, tm=tm, seq_len=seq_len)

    pl.when(i < n_ctx)(
        functools.partial(run, ctx_in, outc_ref, i, tm_ctx, t_ctx))
    pl.when(i >= n_ctx)(
        functools.partial(run, lat_in, outl_ref, i - n_ctx, tm_lat, t_lat))


def _ffn(xc, xl, mod, ng, w_up, conv_w, conv_b, w_down, *, layer, t_ctx, t_lat, ctx_row,
         oc=None, ol=None, w_o=None):
    d = xc.shape[1]
    fuse = oc is not None
    tm_ctx, tm_lat = min(TOKEN_TILE, t_ctx), min(TOKEN_TILE, t_lat)
    g = _TwoGroups(xc.shape[0] // tm_ctx, xl.shape[0] // tm_lat, tm_lat, t_lat, ctx_row)

    def group_specs(n_rows, tm, tile):
        hb = tm // HALO
        last = n_rows // HALO - 1
        return [pl.BlockSpec((tm, d), lambda i: (tile(i), 0)),
                pl.BlockSpec((HALO, d), lambda i: (jnp.maximum(tile(i) * hb - 1, 0), 0)),
                pl.BlockSpec((HALO, d), lambda i: (jnp.minimum((tile(i) + 1) * hb, last), 0))]

    args, specs = [], []
    for x, o, tm, tile in ((xc, oc, tm_ctx, g.ctx), (xl, ol, tm_lat, g.lat)):
        for a in ([x, o] if fuse else [x]):
            args += [a, a, a]
            specs += group_specs(x.shape[0], tm, tile)
    consts = ([w_o] if fuse else []) + [mod, ng, w_up, conv_w, conv_b, w_down]
    args += consts
    for a in consts:
        if a is mod:
            specs.append(g.mod_spec(d))
        else:
            specs.append(_const_spec(a.shape, layer if a is w_up or a is w_down else None))
    return pl.pallas_call(
        functools.partial(_ffn_kernel, n_ctx=g.n_ctx, tm_ctx=tm_ctx, t_ctx=t_ctx,
                          tm_lat=tm_lat, t_lat=t_lat, fuse_oproj=fuse),
        grid=(g.n_ctx + g.n_lat,),
        in_specs=specs,
        out_specs=[pl.BlockSpec((tm_ctx, d), lambda i: (g.ctx(i), 0)),
                   pl.BlockSpec((tm_lat, d), lambda i: (g.lat(i), 0))],
        out_shape=[jax.ShapeDtypeStruct(xc.shape, F32), jax.ShapeDtypeStruct(xl.shape, F32)],
        scratch_shapes=[pltpu.VMEM((max(tm_ctx, tm_lat) + HALO, d), BF16),
                        pltpu.VMEM((max(tm_ctx, tm_lat), d), F32)],
        compiler_params=_params("arbitrary"),
        name="ffn_oproj" if fuse else "ffn",
    )(*args)


def _store_keys(k_ref, kn, krb):
    for h in range(N_HEADS):
        lo = h * HEAD_PAD
        k_ref[:, lo:lo + NOPE_DIM] = kn[:, h * NOPE_DIM:(h + 1) * NOPE_DIM].astype(BF16)
        k_ref[:, lo + NOPE_DIM:lo + HEAD_PAD] = krb


def _store_values_t(vt_ref, wuvt_ref, ckvb):
    for lo, hi in _row_chunks(N_HEADS * V_DIM):
        vt_ref[0, lo:hi, :] = _dot_nt(wuvt_ref[lo:hi, :], ckvb).astype(BF16)


def _mla_proj_kernel(*refs, rope):
    (x_ref, mod_ref, ng_ref, wdown_ref, gkv_ref, gq_ref, wuq_ref,
     wuk_ref, wuvt_ref, *refs) = refs
    if rope:
        cos_ref, sin_ref, q_ref, k_ref, vt_ref = refs
        cos = cos_ref[...]
        sin = sin_ref[...]
    else:
        q_ref, k_ref, vt_ref, ckv_ref, kr_ref = refs
    m = mod_ref[0]
    ng = ng_ref[...]
    hb = (_rms(x_ref[...], ng[0:1] * (1.0 + m[1:2])) + m[0:1]).astype(BF16)
    down = _mdot(hb, wdown_ref[...])
    ckv = _rms(down[:, :KV_RANK], gkv_ref[...])
    kr = down[:, KV_RANK:KV_RANK + ROPE_PAD]
    if rope:
        kr = kr * cos + pltpu.roll(kr, ROPE_DIM, axis=1) * sin
    else:
        ckv_ref[...] = ckv
        kr_ref[...] = kr[:, :ROPE_DIM]
    ql = _rms(down[:, KV_RANK + ROPE_PAD:], gq_ref[...]).astype(BF16)
    q = _mdot(ql, wuq_ref[...])
    ckvb = ckv.astype(BF16)
    _store_values_t(vt_ref, wuvt_ref, ckvb)
    _store_keys(k_ref, _mdot(ckvb, wuk_ref[...]), kr.astype(BF16))
    for h in range(N_HEADS):
        lo = h * HEAD_PAD
        mid = lo + NOPE_DIM
        q_ref[:, lo:mid] = (q[:, lo:mid] * QK_SCALE_LOG2E).astype(BF16)
        qr = q[:, mid:lo + HEAD_PAD]
        if rope:
            qr = qr * cos + pltpu.roll(qr, ROPE_DIM, axis=1) * sin
        q_ref[:, mid:lo + HEAD_PAD] = (qr * QK_SCALE_LOG2E).astype(BF16)


def _mla_proj(x, mod, ng, w, *, tm, seq_len, mod_row, rope_tables=None):
    n, d = x.shape
    rope = rope_tables is not None
    tiles_per_seq = seq_len // tm
    consts = [ng, w["down_rope"] if rope else w["down"], w["g_kv"], w["g_q"],
              w["uq_rope"] if rope else w["uq"], w["uk"], w["uv_t"]]
    args = [x, mod] + consts
    specs = [pl.BlockSpec((tm, d), lambda i: (i, 0)),
             pl.BlockSpec((1, 6, d), lambda i: (mod_row(i * tm), 0, 0))]
    specs += [_const_spec(a.shape) for a in consts]
    row = lambda i: (i, 0)
    col = lambda i: (i // tiles_per_seq, 0, i % tiles_per_seq)
    out_shape = [jax.ShapeDtypeStruct((n, N_HEADS * HEAD_PAD), BF16),
                 jax.ShapeDtypeStruct((n, N_HEADS * HEAD_PAD), BF16),
                 jax.ShapeDtypeStruct((n // seq_len, N_HEADS * V_DIM, seq_len), BF16)]
    out_specs = [pl.BlockSpec((tm, N_HEADS * HEAD_PAD), row),
                 pl.BlockSpec((tm, N_HEADS * HEAD_PAD), row),
                 pl.BlockSpec((1, N_HEADS * V_DIM, tm), col)]
    if rope:
        args += list(rope_tables)
        specs += [pl.BlockSpec((tm, ROPE_PAD), lambda i: (i % tiles_per_seq, 0))] * 2
    else:
        out_shape += [jax.ShapeDtypeStruct((n, KV_RANK), F32),
                      jax.ShapeDtypeStruct((n, ROPE_DIM), F32)]
        out_specs += [pl.BlockSpec((tm, KV_RANK), row),
                      pl.BlockSpec((tm, ROPE_DIM), row)]
    return pl.pallas_call(
        functools.partial(_mla_proj_kernel, rope=rope),
        grid=(n // tm,),
        in_specs=specs,
        out_specs=out_specs,
        out_shape=out_shape,
        compiler_params=_params("parallel"),
        name="mla_proj_rope" if rope else "mla_proj",
    )(*args)


def _kv_expand_kernel(ckv_ref, kr_ref, wuk_ref, wuvt_ref, k_ref, vt_ref):
    ckvb = ckv_ref[0].astype(BF16)
    _store_values_t(vt_ref, wuvt_ref, ckvb)
    _store_keys(k_ref.at[0], _mdot(ckvb, wuk_ref[...]), kr_ref[0].astype(BF16))


def _kv_expand(ckv, kr_pad, w_uk, w_uv_t):
    b, s, _ = ckv.shape
    blk = lambda i: (i, 0, 0)
    return pl.pallas_call(
        _kv_expand_kernel,
        grid=(b,),
        in_specs=[pl.BlockSpec((1, s, KV_RANK), blk),
                  pl.BlockSpec((1, s, ROPE_PAD), blk),
                  _const_spec(w_uk.shape), _const_spec(w_uv_t.shape)],
        out_specs=[pl.BlockSpec((1, s, N_HEADS * HEAD_PAD), blk),
                   pl.BlockSpec((1, N_HEADS * V_DIM, s), blk)],
        out_shape=[jax.ShapeDtypeStruct((b, s, N_HEADS * HEAD_PAD), BF16),
                   jax.ShapeDtypeStruct((b, N_HEADS * V_DIM, s), BF16)],
        compiler_params=_params("parallel"),
        name="kv_expand",
    )(ckv, kr_pad, w_uk, w_uv_t)


def _attn_kernel(*refs, n_kv, heads, t):
    q_ref = refs[0]
    kv_refs = refs[1:1 + 2 * n_kv]
    o_ref = refs[1 + 2 * n_kv]
    items = [(h, c) for h in range(heads) for c in range(t // Q_CHUNK)]

    def key_blocks(s):
        n_keys = kv_refs[2 * s].shape[1]
        step = min(KEY_CHUNK, n_keys)
        return [(r, r + step) for r in range(0, n_keys, step)]

    def scores(item):
        h, c = item
        q = q_ref[0, c * Q_CHUNK:(c + 1) * Q_CHUNK, h * HEAD_PAD:(h + 1) * HEAD_PAD]
        return [_dot_nt(kv_refs[2 * s][0, lo:hi, h * HEAD_PAD:(h + 1) * HEAD_PAD], q)
                for s in range(n_kv) for lo, hi in key_blocks(s)]

    def softmax_pv(item, blocks):
        h, c = item
        mx = functools.reduce(
            jnp.maximum, [jnp.max(b, axis=0, keepdims=True) for b in blocks])
        ps = [jnp.exp2(b - mx) for b in blocks]
        den = functools.reduce(
            jnp.add, [jnp.sum(p, axis=0, keepdims=True) for p in ps])
        acc = None
        first = 0
        for s in range(n_kv):
            nb = len(key_blocks(s))
            pt = jnp.concatenate([p.astype(BF16) for p in ps[first:first + nb]], axis=0)
            first += nb
            d = _dot(kv_refs[2 * s + 1][0, h * V_DIM:(h + 1) * V_DIM, :], pt)
            acc = d if acc is None else acc + d
        o = (acc * (1.0 / den)).T
        o_ref[0, c * Q_CHUNK:(c + 1) * Q_CHUNK, h * V_DIM:(h + 1) * V_DIM] = o.astype(BF16)

    nxt = scores(items[0])
    for n, item in enumerate(items):
        cur = nxt
        if n + 1 < len(items):
            nxt = scores(items[n + 1])
        softmax_pv(item, cur)


def _attention(q, kvs, *, heads):
    b, t, _ = q.shape
    args = [q]
    specs = [pl.BlockSpec((1, t, heads * HEAD_PAD), lambda bi, h: (bi, 0, h))]
    for k, vt in kvs:
        s = k.shape[1]
        args += [k, vt]
        specs += [pl.BlockSpec((1, s, heads * HEAD_PAD), lambda bi, h: (bi, 0, h)),
                  pl.BlockSpec((1, heads * V_DIM, s), lambda bi, h: (bi, h, 0))]
    return pl.pallas_call(
        functools.partial(_attn_kernel, n_kv=len(kvs), heads=heads, t=t),
        grid=(b, N_HEADS // heads),
        in_specs=specs,
        out_specs=pl.BlockSpec((1, t, heads * V_DIM), lambda bi, h: (bi, 0, h)),
        out_shape=jax.ShapeDtypeStruct((b, t, N_HEADS * V_DIM), BF16),
        compiler_params=_params("parallel", "parallel"),
        name="attention",
    )(*args)


_ROPE_SWAP = np.concatenate([np.arange(16, 32), np.arange(0, 16),
                             np.arange(48, 64), np.arange(32, 48)])


def _mla_weights(w_dq, g_q, w_uq, w_dkv, g_kv, w_uk, w_uv):
    zpad = jnp.zeros((D_MODEL, ROPE_PAD - ROPE_DIM), F32)
    k_rope = w_dkv[:, KV_RANK:]
    down = jnp.concatenate([w_dkv, zpad, w_dq], axis=1)
    down_rope = jnp.concatenate([w_dkv, k_rope[:, _ROPE_SWAP], w_dq], axis=1)
    uq = w_uq.reshape(Q_RANK, N_HEADS, NOPE_DIM + ROPE_DIM)
    q_rope = uq[:, :, NOPE_DIM:]
    hpad = jnp.zeros((Q_RANK, N_HEADS, ROPE_PAD - ROPE_DIM), F32)
    uq_full = jnp.concatenate([uq, hpad], axis=2).reshape(Q_RANK, N_HEADS * HEAD_PAD)
    uq_rope = jnp.concatenate([uq, q_rope[:, :, _ROPE_SWAP]], axis=2)
    uq_rope = uq_rope.reshape(Q_RANK, N_HEADS * HEAD_PAD)
    return dict(down=down.astype(BF16), down_rope=down_rope.astype(BF16),
                g_kv=g_kv.reshape(1, -1), g_q=g_q.reshape(1, -1),
                uq=uq_full.astype(BF16), uq_rope=uq_rope.astype(BF16),
                uk=w_uk.astype(BF16), uv_t=w_uv.T.astype(BF16))


def _rope_tables(t):
    half = ROPE_DIM // 4
    pos = np.arange(t)
    inv = ROPE_BASE ** (-np.arange(half, dtype=np.float64) / half)
    ang_r = (pos // GRID_W)[:, None] * inv[None, :]
    ang_c = (pos % GRID_W)[:, None] * inv[None, :]
    zeros = np.zeros((t, ROPE_PAD - ROPE_DIM))
    cos = np.concatenate([np.cos(ang_r), np.cos(ang_r), np.cos(ang_c),
                          np.cos(ang_c), zeros], axis=1)
    sin = np.concatenate([-np.sin(ang_r), np.sin(ang_r), -np.sin(ang_c),
                          np.sin(ang_c), zeros], axis=1)
    return jnp.asarray(cos, F32), jnp.asarray(sin, F32)


def kernel(x_prompt, x_sample, cache_ckv, cache_krope, c, c_ctx, ada_w, ada_b, norm_g, gm_w_in, gm_b_in, gm_g_v, gm_w_s, gm_b_s, gm_w_out, mla_w_dq, mla_g_q, mla_w_uq, mla_w_dkv, mla_g_kv, mla_w_uk, mla_w_uv, mla_w_o, ffn_w_up, ffn_conv_w, ffn_conv_b, ffn_w_down):
    nb_ctx, t_ctx, d = x_prompt.shape
    nb_lat, t_lat, _ = x_sample.shape
    ctx_row = nb_lat

    cond = jnp.concatenate(
        [c, c_ctx[None, :], jnp.zeros((N_COND - nb_lat - 1, d), F32)], axis=0)
    mod = _ada(cond, ada_w, ada_b).reshape(ada_w.shape[0], N_COND, 6, d)

    mixer_w = (gm_w_in[0].astype(BF16), gm_b_in[0].reshape(1, -1),
               gm_g_v[0].reshape(1, -1), gm_w_s[0].astype(BF16), gm_b_s[0].T,
               gm_w_out[0].astype(BF16))
    mla_w = _mla_weights(mla_w_dq[0], mla_g_q[0], mla_w_uq[0], mla_w_dkv[0],
                         mla_g_kv[0], mla_w_uk[0], mla_w_uv[0])

    groups = dict(t_ctx=t_ctx, t_lat=t_lat, ctx_row=ctx_row)
    xc = x_prompt.reshape(nb_ctx * t_ctx, d)
    xl = x_sample.reshape(nb_lat * t_lat, d)
    depth, _, up_cols = ffn_w_up.shape
    xc, xl, w_up, w_down, w_o = _mixer(
        xc, xl, mod[0], norm_g[0], *mixer_w, tm=TOKEN_TILE, t_lat=t_lat, ctx_row=ctx_row,
        casts=[ffn_w_up.reshape(depth * d, up_cols), ffn_w_down.reshape(depth * D_FF, d),
               mla_w_o[0]])
    w_up = w_up.reshape(depth, d, up_cols)
    w_down = w_down.reshape(depth, D_FF, d)
    ffn_w = [(w_up, ffn_conv_w[i], ffn_conv_b[i].reshape(1, -1), w_down) for i in range(depth)]
    xc, xl = _ffn(xc, xl, mod[0], norm_g[0], *ffn_w[0], layer=0, **groups)

    def attend(x, nb, t, mod_row, heads, rope_tables, cached):
        proj = _mla_proj(x, mod[1], norm_g[1], mla_w, tm=min(TOKEN_TILE, t), seq_len=t,
                         mod_row=mod_row, rope_tables=rope_tables)
        q, k = (a.reshape(nb, t, -1) for a in proj[:2])
        kvs = [(k, proj[2])] + ([cached] if cached is not None else [])
        return _attention(q, kvs, heads=heads).reshape(nb * t, -1), proj[3:]

    oc, (ckv, krope) = attend(xc, nb_ctx, t_ctx, lambda row: ctx_row, N_HEADS, None, None)
    kr_pad = jnp.pad(cache_krope[:, 0], ((0, 0), (0, 0), (0, ROPE_PAD - ROPE_DIM)))
    cached = _kv_expand(cache_ckv[:, 0], kr_pad, mla_w["uk"], mla_w["uv_t"])
    ol, _ = attend(xl, nb_lat, t_lat, lambda row: row // t_lat, LATENT_HEADS_PER_STEP,
                   _rope_tables(t_lat), cached)
    xc, xl = _ffn(xc, xl, mod[1], norm_g[1], *ffn_w[1], layer=1, **groups,
                  oc=oc, ol=ol, w_o=w_o)

    return (xc.reshape(nb_ctx, t_ctx, d), xl.reshape(nb_lat, t_lat, d),
            ckv.reshape(nb_ctx, 1, t_ctx, KV_RANK),
            krope.reshape(nb_ctx, 1, t_ctx, ROPE_DIM))
```

```python
import functools

import jax
import jax.numpy as jnp
import numpy as np
from jax import lax
from jax.experimental import pallas as pl
from jax.experimental.pallas import tpu as pltpu

D_MODEL = 1024
GRID_W = 64
CHUNK = 128
D_INNER = 2 * D_MODEL
N_GROUPS = 8
GROUP_DIM = D_INNER // N_GROUPS
N_HEADS = 8
NOPE_DIM = 128
ROPE_DIM = 64
V_DIM = 128
Q_RANK = 384
KV_RANK = 256
ROPE_BASE = 10000.0
D_FF = 2816
EPS = 1e-6

LANES = 128
BF16_SUBLANES = 16
MXU_DIM = 256
VMEM_LIMIT_BYTES = 56 * 1024 * 1024

HEAD_PAD = 2 * LANES
ROPE_PAD = LANES
N_COND = 16
HALO = BF16_SUBLANES
FF_TILE = MXU_DIM
TOKEN_TILE = 512
ADA_COL_TILE = 2048
LATENT_HEADS_PER_STEP = 2
MAX_DOT_ROWS = 176
CAST_STEPS = 32
Q_CHUNK = MXU_DIM
KEY_CHUNK = 2 * MXU_DIM
QK_SCALE_LOG2E = float((NOPE_DIM + ROPE_DIM) ** -0.5 * np.log2(np.e))

F32 = jnp.float32
BF16 = jnp.bfloat16


def _dot(a, b):
    return jnp.dot(a, b, preferred_element_type=F32)


def _dot_nt(a, b):
    return lax.dot_general(a, b, (((1,), (1,)), ((), ())),
                           preferred_element_type=F32)


def _row_chunks(rows):
    n = -(-rows // MAX_DOT_ROWS)
    step = -(-rows // (n * BF16_SUBLANES)) * BF16_SUBLANES
    return [(lo, min(lo + step, rows)) for lo in range(0, rows, step)]


def _mdot(a, w):
    return jnp.concatenate(
        [_dot(a[lo:hi], w) for lo, hi in _row_chunks(a.shape[0])], axis=0)


def _rms(x, g):
    return x * lax.rsqrt(jnp.mean(x * x, axis=-1, keepdims=True) + EPS) * g


def _sigmoid(x):
    return 1.0 / (1.0 + jnp.exp(-x))


def _gelu(x):
    return 0.5 * x * (1.0 + lax.erf(x * np.float32(np.sqrt(0.5))))


def _params(*sem):
    return pltpu.CompilerParams(dimension_semantics=sem,
                                vmem_limit_bytes=VMEM_LIMIT_BYTES)


def _const_spec(shape, layer=None):
    if layer is None:
        block, index = shape, (0,) * len(shape)
    else:
        block, index = (None,) + tuple(shape[1:]), (layer,) + (0,) * (len(shape) - 1)
    return pl.BlockSpec(block, lambda *_: index, pipeline_mode=pl.Buffered(1))


def _ada_kernel(cond_ref, w_ref, b_ref, o_ref):
    c = cond_ref[...]
    s = (c * _sigmoid(c)).astype(BF16)
    o_ref[0] = _dot(s, w_ref[0].astype(BF16)) + b_ref[0]


def _ada(cond, ada_w, ada_b):
    depth, d, n = ada_w.shape
    tn = ADA_COL_TILE
    return pl.pallas_call(
        _ada_kernel,
        grid=(depth, n // tn),
        in_specs=[
            pl.BlockSpec((N_COND, d), lambda i, j: (0, 0)),
            pl.BlockSpec((1, d, tn), lambda i, j: (i, 0, j)),
            pl.BlockSpec((1, 1, tn), lambda i, j: (i, 0, j)),
        ],
        out_specs=pl.BlockSpec((1, N_COND, tn), lambda i, j: (i, 0, j)),
        out_shape=jax.ShapeDtypeStruct((depth, N_COND, n), F32),
        compiler_params=_params("parallel", "parallel"),
        name="ada",
    )(cond, ada_w, ada_b.reshape(depth, 1, n))


def _mixer_tile(x_ref, o_ref, mod_ref, ng_ref, w_in_ref, b_in_ref, gv_ref, ws_ref,
                bs_ref, w_out_ref, *, tm, casts):
    m = mod_ref[0]
    ng = ng_ref[...]
    x = x_ref[...]
    hb = (_rms(x, ng[0:1] * (1.0 + m[1:2])) + m[0:1]).astype(BF16)
    n_chunks = tm // CHUNK

    def pre(c):
        lo = c * GROUP_DIM
        return _mdot(hb, w_in_ref[:, lo:lo + GROUP_DIM]) + b_in_ref[:, lo:lo + GROUP_DIM]

    vs = []
    ss = None
    nxt = pre(N_GROUPS)
    for c in range(N_GROUPS):
        cur = nxt
        if c + 1 < N_GROUPS:
            nxt = pre(N_GROUPS + c + 1)
        v = _gelu(cur)
        s = jnp.sum(v * v, axis=-1, keepdims=True)
        ss = s if ss is None else ss + s
        vs.append(v)
    rs = lax.rsqrt(ss * (1.0 / D_INNER) + EPS)

    for src_ref, dst_ref in casts:
        dst_ref[...] = src_ref[...].astype(BF16)

    def mix(g):
        lo = g * GROUP_DIM
        vg = (vs[g] * rs * gv_ref[:, lo:lo + GROUP_DIM]).astype(BF16)
        w_s = ws_ref[g]
        return jnp.concatenate(
            [_dot(w_s, vg[n * CHUNK:(n + 1) * CHUNK]) for n in range(n_chunks)],
            axis=0)

    def down(t, g, acc):
        d = _mdot(t, w_out_ref[g * GROUP_DIM:(g + 1) * GROUP_DIM, :])
        return d if acc is None else acc + d

    acc = None
    t_prev = None
    nxt = (pre(0), mix(0))
    for g in range(N_GROUPS):
        u_pre, mixed = nxt
        if g + 1 < N_GROUPS:
            nxt = (pre(g + 1), mix(g + 1))
        if t_prev is not None:
            acc = down(t_prev, g - 1, acc)
        b_s = jnp.concatenate([bs_ref[:, g:g + 1]] * n_chunks, axis=0)
        t_prev = (_gelu(u_pre) * (mixed + b_s)).astype(BF16)
    acc = down(t_prev, N_GROUPS - 1, acc)
    o_ref[...] = x + _rms(acc, m[2:3] * ng[1:2])


def _mixer_kernel(xc_ref, xl_ref, *refs, tm, n_ctx, n_cast):
    ins, (oc_ref, ol_ref, *cast_out) = refs[:-(2 + n_cast)], refs[-(2 + n_cast):]
    shared, cast_in = ins[:len(ins) - n_cast], ins[len(ins) - n_cast:]
    tile = functools.partial(_mixer_tile, tm=tm, casts=tuple(zip(cast_in, cast_out)))
    i = pl.program_id(0)
    pl.when(i < n_ctx)(functools.partial(tile, xc_ref, oc_ref, *shared))
    pl.when(i >= n_ctx)(functools.partial(tile, xl_ref, ol_ref, *shared))


class _TwoGroups:
    def __init__(self, n_ctx, n_lat, tm_lat, t_lat, ctx_row):
        self.n_ctx, self.n_lat = n_ctx, n_lat
        self.tm_lat, self.t_lat, self.ctx_row = tm_lat, t_lat, ctx_row

    def ctx(self, i):
        return jnp.minimum(i, self.n_ctx - 1)

    def lat(self, i):
        return jnp.maximum(i - self.n_ctx, 0)

    def mod_spec(self, d):
        def index(i):
            lat_row = self.lat(i) * self.tm_lat // self.t_lat
            return (jnp.where(i < self.n_ctx, self.ctx_row, lat_row), 0, 0)
        return pl.BlockSpec((1, 6, d), index)


def _mixer(xc, xl, mod, ng, w_in, b_in, g_v, w_s, b_s_t, w_out, *, tm, t_lat, ctx_row,
           casts):
    d = xc.shape[1]
    g = _TwoGroups(xc.shape[0] // tm, xl.shape[0] // tm, tm, t_lat, ctx_row)
    n_steps = g.n_ctx + g.n_lat
    ctx_spec = pl.BlockSpec((tm, d), lambda i: (g.ctx(i), 0))
    lat_spec = pl.BlockSpec((tm, d), lambda i: (g.lat(i), 0))
    consts = [ng, w_in, b_in, g_v, w_s, b_s_t, w_out]

    def cast_spec(a):
        rows, cols = a.shape
        assert rows % (CAST_STEPS * BF16_SUBLANES) == 0 and CAST_STEPS <= n_steps
        return pl.BlockSpec((rows // CAST_STEPS, cols),
                            lambda i: (jnp.minimum(i, CAST_STEPS - 1), 0))

    cast_specs = [cast_spec(a) for a in casts]
    return pl.pallas_call(
        functools.partial(_mixer_kernel, tm=tm, n_ctx=g.n_ctx, n_cast=len(casts)),
        grid=(n_steps,),
        in_specs=([ctx_spec, lat_spec, g.mod_spec(d)]
                  + [_const_spec(a.shape) for a in consts] + cast_specs),
        out_specs=[ctx_spec, lat_spec] + cast_specs,
        out_shape=([jax.ShapeDtypeStruct(xc.shape, F32), jax.ShapeDtypeStruct(xl.shape, F32)]
                   + [jax.ShapeDtypeStruct(a.shape, BF16) for a in casts]),
        compiler_params=_params("arbitrary"),
        name="mixer",
    )(xc, xl, mod, *consts, *casts)


def _ffn_tile(x_refs, o_refs, out_ref, tile, shared, *, tm, seq_len):
    (wo_ref, mod_ref, ng_ref, wup_ref, cw_ref, cb_ref, wd_ref, hn_scr, x1_scr) = shared
    fuse_oproj = o_refs is not None
    x_ref, xp_ref, xn_ref = x_refs
    o_ref, op_ref, on_ref = o_refs if fuse_oproj else (None, None, None)
    m = mod_ref[0]
    ng = ng_ref[...]
    n_stages = D_FF // FF_TILE
    rows = tm + HALO

    x1_all = jnp.concatenate([xp_ref[...], x_ref[...], xn_ref[...]], axis=0)
    if fuse_oproj:
        o_all = jnp.concatenate([op_ref[...], o_ref[...], on_ref[...]], axis=0)
        x1_all = x1_all + _rms(_mdot(o_all, wo_ref[...]), m[2:3] * ng[1:2])
    hn_all = _rms(x1_all, ng[2:3] * (1.0 + m[4:5])) + m[3:4]
    x1_scr[0:tm] = x1_all[HALO:HALO + tm]
    hn_scr[HALO:rows] = hn_all[HALO:HALO + tm].astype(BF16)
    row0 = tile * tm
    at_start = (row0 % seq_len) == 0
    at_end = ((row0 + tm) % seq_len) == 0
    hp = hn_all[:HALO]
    hp = jnp.where(at_start, jnp.zeros_like(hp), hp)
    hx = hn_all[HALO + tm:]
    hx = jnp.where(at_end, jnp.zeros_like(hx), hx)
    rid = lax.broadcasted_iota(jnp.int32, hp.shape, 0)
    hn_scr[0:HALO] = jnp.where(rid == 0, hx, hp).astype(BF16)
    hn = hn_scr[0:rows]

    def up_half(j, half):
        lo = half * D_FF + j * FF_TILE
        return _mdot(hn, wup_ref[:, lo:lo + FF_TILE])

    def conv(h, lo):
        cw = cw_ref[:, lo:lo + FF_TILE]
        return (pltpu.roll(h, 1, axis=0)[HALO:] * cw[0:1] + h[HALO:] * cw[1:2]
                + pltpu.roll(h, rows - 1, axis=0)[HALO:] * cw[2:3]
                + cb_ref[:, lo:lo + FF_TILE])

    def down(t, j, acc):
        d = _mdot(t, wd_ref[j * FF_TILE:(j + 1) * FF_TILE, :])
        return d if acc is None else acc + d

    nxt = (up_half(0, 0), up_half(0, 1))
    acc = None
    t_prev = None
    for j in range(n_stages):
        hg, hv = nxt
        more = j + 1 < n_stages
        next_g = up_half(j + 1, 0) if more else None
        if t_prev is not None:
            acc = down(t_prev, j - 1, acc)
        cg = conv(hg, j * FF_TILE)
        next_v = up_half(j + 1, 1) if more else None
        cv = conv(hv, D_FF + j * FF_TILE)
        t_prev = (cg * _sigmoid(cg) * cv).astype(BF16)
        nxt = (next_g, next_v)
    acc = down(t_prev, n_stages - 1, acc)
    out_ref[...] = x1_scr[0:tm] + _rms(acc, m[5:6] * ng[3:4])


def _ffn_kernel(*refs, n_ctx, tm_ctx, t_ctx, tm_lat, t_lat, fuse_oproj):
    per_group = 6 if fuse_oproj else 3
    ctx_in, lat_in = refs[:per_group], refs[per_group:2 * per_group]
    refs = refs[2 * per_group:]
    if not fuse_oproj:
        refs = (None,) + refs
    *weights, outc_ref, outl_ref, hn_scr, x1_scr = refs
    shared = (*weights, hn_scr, x1_scr)
    i = pl.program_id(0)

    def run(group_in, out_ref, tile, tm, seq_len):
        o_refs = group_in[3:] if fuse_oproj else None
        _ffn_tile(group_in[:3], o_refs, out_ref, tile, shared, tm=tm, seq_len=seq_len)

    pl.when(i < n_ctx)(
        functools.partial(run, ctx_in, outc_ref, i, tm_ctx, t_ctx))
    pl.when(i >= n_ctx)(
        functools.partial(run, lat_in, outl_ref, i - n_ctx, tm_lat, t_lat))


def _ffn(xc, xl, mod, ng, w_up, conv_w, conv_b, w_down, *, layer, t_ctx, t_lat, ctx_row,
         oc=None, ol=None, w_o=None):
    d = xc.shape[1]
    fuse = oc is not None
    tm_ctx, tm_lat = min(TOKEN_TILE, t_ctx), min(TOKEN_TILE, t_lat)
    g = _TwoGroups(xc.shape[0] // tm_ctx, xl.shape[0] // tm_lat, tm_lat, t_lat, ctx_row)

    def group_specs(n_rows, tm, tile):
        hb = tm // HALO
        last = n_rows // HALO - 1
        return [pl.BlockSpec((tm, d), lambda i: (tile(i), 0)),
                pl.BlockSpec((HALO, d), lambda i: (jnp.maximum(tile(i) * hb - 1, 0), 0)),
                pl.BlockSpec((HALO, d), lambda i: (jnp.minimum((tile(i) + 1) * hb, last), 0))]

    args, specs = [], []
    for x, o, tm, tile in ((xc, oc, tm_ctx, g.ctx), (xl, ol, tm_lat, g.lat)):
        for a in ([x, o] if fuse else [x]):
            args += [a, a, a]
            specs += group_specs(x.shape[0], tm, tile)
    consts = ([w_o] if fuse else []) + [mod, ng, w_up, conv_w, conv_b, w_down]
    args += consts
    for a in consts:
        if a is mod:
            specs.append(g.mod_spec(d))
        else:
            specs.append(_const_spec(a.shape, layer if a is w_up or a is w_down else None))
    return pl.pallas_call(
        functools.partial(_ffn_kernel, n_ctx=g.n_ctx, tm_ctx=tm_ctx, t_ctx=t_ctx,
                          tm_lat=tm_lat, t_lat=t_lat, fuse_oproj=fuse),
        grid=(g.n_ctx + g.n_lat,),
        in_specs=specs,
        out_specs=[pl.BlockSpec((tm_ctx, d), lambda i: (g.ctx(i), 0)),
                   pl.BlockSpec((tm_lat, d), lambda i: (g.lat(i), 0))],
        out_shape=[jax.ShapeDtypeStruct(xc.shape, F32), jax.ShapeDtypeStruct(xl.shape, F32)],
        scratch_shapes=[pltpu.VMEM((max(tm_ctx, tm_lat) + HALO, d), BF16),
                        pltpu.VMEM((max(tm_ctx, tm_lat), d), F32)],
        compiler_params=_params("arbitrary"),
        name="ffn_oproj" if fuse else "ffn",
    )(*args)


def _store_keys(k_ref, kn, krb):
    for h in range(N_HEADS):
        lo = h * HEAD_PAD
        k_ref[:, lo:lo + NOPE_DIM] = kn[:, h * NOPE_DIM:(h + 1) * NOPE_DIM].astype(BF16)
        k_ref[:, lo + NOPE_DIM:lo + HEAD_PAD] = krb


def _store_values_t(vt_ref, wuvt_ref, ckvb):
    for lo, hi in _row_chunks(N_HEADS * V_DIM):
        vt_ref[0, lo:hi, :] = _dot_nt(wuvt_ref[lo:hi, :], ckvb).astype(BF16)


def _mla_proj_kernel(*refs, rope):
    (x_ref, mod_ref, ng_ref, wdown_ref, gkv_ref, gq_ref, wuq_ref,
     wuk_ref, wuvt_ref, *refs) = refs
    if rope:
        cos_ref, sin_ref, q_ref, k_ref, vt_ref = refs
        cos = cos_ref[...]
        sin = sin_ref[...]
    else:
        q_ref, k_ref, vt_ref, ckv_ref, kr_ref = refs
    m = mod_ref[0]
    ng = ng_ref[...]
    hb = (_rms(x_ref[...], ng[0:1] * (1.0 + m[1:2])) + m[0:1]).astype(BF16)
    down = _mdot(hb, wdown_ref[...])
    ckv = _rms(down[:, :KV_RANK], gkv_ref[...])
    kr = down[:, KV_RANK:KV_RANK + ROPE_PAD]
    if rope:
        kr = kr * cos + pltpu.roll(kr, ROPE_DIM, axis=1) * sin
    else:
        ckv_ref[...] = ckv
        kr_ref[...] = kr[:, :ROPE_DIM]
    ql = _rms(down[:, KV_RANK + ROPE_PAD:], gq_ref[...]).astype(BF16)
    q = _mdot(ql, wuq_ref[...])
    ckvb = ckv.astype(BF16)
    _store_values_t(vt_ref, wuvt_ref, ckvb)
    _store_keys(k_ref, _mdot(ckvb, wuk_ref[...]), kr.astype(BF16))
    for h in range(N_HEADS):
        lo = h * HEAD_PAD
        mid = lo + NOPE_DIM
        q_ref[:, lo:mid] = (q[:, lo:mid] * QK_SCALE_LOG2E).astype(BF16)
        qr = q[:, mid:lo + HEAD_PAD]
        if rope:
            qr = qr * cos + pltpu.roll(qr, ROPE_DIM, axis=1) * sin
        q_ref[:, mid:lo + HEAD_PAD] = (qr * QK_SCALE_LOG2E).astype(BF16)


def _mla_proj(x, mod, ng, w, *, tm, seq_len, mod_row, rope_tables=None):
    n, d = x.shape
    rope = rope_tables is not None
    tiles_per_seq = seq_len // tm
    consts = [ng, w["down_rope"] if rope else w["down"], w["g_kv"], w["g_q"],
              w["uq_rope"] if rope else w["uq"], w["uk"], w["uv_t"]]
    args = [x, mod] + consts
    specs = [pl.BlockSpec((tm, d), lambda i: (i, 0)),
             pl.BlockSpec((1, 6, d), lambda i: (mod_row(i * tm), 0, 0))]
    specs += [_const_spec(a.shape) for a in consts]
    row = lambda i: (i, 0)
    col = lambda i: (i // tiles_per_seq, 0, i % tiles_per_seq)
    out_shape = [jax.ShapeDtypeStruct((n, N_HEADS * HEAD_PAD), BF16),
                 jax.ShapeDtypeStruct((n, N_HEADS * HEAD_PAD), BF16),
                 jax.ShapeDtypeStruct((n // seq_len, N_HEADS * V_DIM, seq_len), BF16)]
    out_specs = [pl.BlockSpec((tm, N_HEADS * HEAD_PAD), row),
                 pl.BlockSpec((tm, N_HEADS * HEAD_PAD), row),
                 pl.BlockSpec((1, N_HEADS * V_DIM, tm), col)]
    if rope:
        args += list(rope_tables)
        specs += [pl.BlockSpec((tm, ROPE_PAD), lambda i: (i % tiles_per_seq, 0))] * 2
    else:
        out_shape += [jax.ShapeDtypeStruct((n, KV_RANK), F32),
                      jax.ShapeDtypeStruct((n, ROPE_DIM), F32)]
        out_specs += [pl.BlockSpec((tm, KV_RANK), row),
                      pl.BlockSpec((tm, ROPE_DIM), row)]
    return pl.pallas_call(
        functools.partial(_mla_proj_kernel, rope=rope),
        grid=(n // tm,),
        in_specs=specs,
        out_specs=out_specs,
        out_shape=out_shape,
        compiler_params=_params("parallel"),
        name="mla_proj_rope" if rope else "mla_proj",
    )(*args)


def _kv_expand_kernel(ckv_ref, kr_ref, wuk_ref, wuvt_ref, k_ref, vt_ref):
    ckvb = ckv_ref[0].astype(BF16)
    _store_values_t(vt_ref, wuvt_ref, ckvb)
    _store_keys(k_ref.at[0], _mdot(ckvb, wuk_ref[...]), kr_ref[0].astype(BF16))


def _kv_expand(ckv, kr_pad, w_uk, w_uv_t):
    b, s, _ = ckv.shape
    blk = lambda i: (i, 0, 0)
    return pl.pallas_call(
        _kv_expand_kernel,
        grid=(b,),
        in_specs=[pl.BlockSpec((1, s, KV_RANK), blk),
                  pl.BlockSpec((1, s, ROPE_PAD), blk),
                  _const_spec(w_uk.shape), _const_spec(w_uv_t.shape)],
        out_specs=[pl.BlockSpec((1, s, N_HEADS * HEAD_PAD), blk),
                   pl.BlockSpec((1, N_HEADS * V_DIM, s), blk)],
        out_shape=[jax.ShapeDtypeStruct((b, s, N_HEADS * HEAD_PAD), BF16),
                   jax.ShapeDtypeStruct((b, N_HEADS * V_DIM, s), BF16)],
        compiler_params=_params("parallel"),
        name="kv_expand",
    )(ckv, kr_pad, w_uk, w_uv_t)


def _attn_kernel(*refs, n_kv, heads, t):
    q_ref = refs[0]
    kv_refs = refs[1:1 + 2 * n_kv]
    o_ref = refs[1 + 2 * n_kv]
    items = [(h, c) for h in range(heads) for c in range(t // Q_CHUNK)]

    def key_blocks(s):
        n_keys = kv_refs[2 * s].shape[1]
        step = min(KEY_CHUNK, n_keys)
        return [(r, r + step) for r in range(0, n_keys, step)]

    def scores(item):
        h, c = item
        q = q_ref[0, c * Q_CHUNK:(c + 1) * Q_CHUNK, h * HEAD_PAD:(h + 1) * HEAD_PAD]
        return [_dot_nt(kv_refs[2 * s][0, lo:hi, h * HEAD_PAD:(h + 1) * HEAD_PAD], q)
                for s in range(n_kv) for lo, hi in key_blocks(s)]

    def softmax_pv(item, blocks):
        h, c = item
        mx = functools.reduce(
            jnp.maximum, [jnp.max(b, axis=0, keepdims=True) for b in blocks])
        ps = [jnp.exp2(b - mx) for b in blocks]
        den = functools.reduce(
            jnp.add, [jnp.sum(p, axis=0, keepdims=True) for p in ps])
        acc = None
        first = 0
        for s in range(n_kv):
            nb = len(key_blocks(s))
            pt = jnp.concatenate([p.astype(BF16) for p in ps[first:first + nb]], axis=0)
            first += nb
            d = _dot(kv_refs[2 * s + 1][0, h * V_DIM:(h + 1) * V_DIM, :], pt)
            acc = d if acc is None else acc + d
        o = (acc * (1.0 / den)).T
        o_ref[0, c * Q_CHUNK:(c + 1) * Q_CHUNK, h * V_DIM:(h + 1) * V_DIM] = o.astype(BF16)

    nxt = scores(items[0])
    for n, item in enumerate(items):
        cur = nxt
        if n + 1 < len(items):
            nxt = scores(items[n + 1])
        softmax_pv(item, cur)


def _attention(q, kvs, *, heads):
    b, t, _ = q.shape
    args = [q]
    specs = [pl.BlockSpec((1, t, heads * HEAD_PAD), lambda bi, h: (bi, 0, h))]
    for k, vt in kvs:
        s = k.shape[1]
        args += [k, vt]
        specs += [pl.BlockSpec((1, s, heads * HEAD_PAD), lambda bi, h: (bi, 0, h)),
                  pl.BlockSpec((1, heads * V_DIM, s), lambda bi, h: (bi, h, 0))]
    return pl.pallas_call(
        functools.partial(_attn_kernel, n_kv=len(kvs), heads=heads, t=t),
        grid=(b, N_HEADS // heads),
        in_specs=specs,
        out_specs=pl.BlockSpec((1, t, heads * V_DIM), lambda bi, h: (bi, 0, h)),
        out_shape=jax.ShapeDtypeStruct((b, t, N_HEADS * V_DIM), BF16),
        compiler_params=_params("parallel", "parallel"),
        name="attention",
    )(*args)


_ROPE_SWAP = np.concatenate([np.arange(16, 32), np.arange(0, 16),
                             np.arange(48, 64), np.arange(32, 48)])


def _mla_weights(w_dq, g_q, w_uq, w_dkv, g_kv, w_uk, w_uv):
    zpad = jnp.zeros((D_MODEL, ROPE_PAD - ROPE_DIM), F32)
    k_rope = w_dkv[:, KV_RANK:]
    down = jnp.concatenate([w_dkv, zpad, w_dq], axis=1)
    down_rope = jnp.concatenate([w_dkv, k_rope[:, _ROPE_SWAP], w_dq], axis=1)
    uq = w_uq.reshape(Q_RANK, N_HEADS, NOPE_DIM + ROPE_DIM)
    q_rope = uq[:, :, NOPE_DIM:]
    hpad = jnp.zeros((Q_RANK, N_HEADS, ROPE_PAD - ROPE_DIM), F32)
    uq_full = jnp.concatenate([uq, hpad], axis=2).reshape(Q_RANK, N_HEADS * HEAD_PAD)
    uq_rope = jnp.concatenate([uq, q_rope[:, :, _ROPE_SWAP]], axis=2)
    uq_rope = uq_rope.reshape(Q_RANK, N_HEADS * HEAD_PAD)
    return dict(down=down.astype(BF16), down_rope=down_rope.astype(BF16),
                g_kv=g_kv.reshape(1, -1), g_q=g_q.reshape(1, -1),
                uq=uq_full.astype(BF16), uq_rope=uq_rope.astype(BF16),
                uk=w_uk.astype(BF16), uv_t=w_uv.T.astype(BF16))


def _rope_tables(t):
    half = ROPE_DIM // 4
    pos = np.arange(t)
    inv = ROPE_BASE ** (-np.arange(half, dtype=np.float64) / half)
    ang_r = (pos // GRID_W)[:, None] * inv[None, :]
    ang_c = (pos % GRID_W)[:, None] * inv[None, :]
    zeros = np.zeros((t, ROPE_PAD - ROPE_DIM))
    cos = np.concatenate([np.cos(ang_r), np.cos(ang_r), np.cos(ang_c),
                          np.cos(ang_c), zeros], axis=1)
    sin = np.concatenate([-np.sin(ang_r), np.sin(ang_r), -np.sin(ang_c),
                          np.sin(ang_c), zeros], axis=1)
    return jnp.asarray(cos, F32), jnp.asarray(sin, F32)


def kernel(x_prompt, x_sample, cache_ckv, cache_krope, c, c_ctx, ada_w, ada_b, norm_g, gm_w_in, gm_b_in, gm_g_v, gm_w_s, gm_b_s, gm_w_out, mla_w_dq, mla_g_q, mla_w_uq, mla_w_dkv, mla_g_kv, mla_w_uk, mla_w_uv, mla_w_o, ffn_w_up, ffn_conv_w, ffn_conv_b, ffn_w_down):
    nb_ctx, t_ctx, d = x_prompt.shape
    nb_lat, t_lat, _ = x_sample.shape
    ctx_row = nb_lat

    cond = jnp.concatenate(
        [c, c_ctx[None, :], jnp.zeros((N_COND - nb_lat - 1, d), F32)], axis=0)
    mod = _ada(cond, ada_w, ada_b).reshape(ada_w.shape[0], N_COND, 6, d)

    mixer_w = (gm_w_in[0].astype(BF16), gm_b_in[0].reshape(1, -1),
               gm_g_v[0].reshape(1, -1), gm_w_s[0].astype(BF16), gm_b_s[0].T,
               gm_w_out[0].astype(BF16))
    mla_w = _mla_weights(mla_w_dq[0], mla_g_q[0], mla_w_uq[0], mla_w_dkv[0],
                         mla_g_kv[0], mla_w_uk[0], mla_w_uv[0])

    groups = dict(t_ctx=t_ctx, t_lat=t_lat, ctx_row=ctx_row)
    xc = x_prompt.reshape(nb_ctx * t_ctx, d)
    xl = x_sample.reshape(nb_lat * t_lat, d)
    depth, _, up_cols = ffn_w_up.shape
    xc, xl, w_up, w_down, w_o = _mixer(
        xc, xl, mod[0], norm_g[0], *mixer_w, tm=TOKEN_TILE, t_lat=t_lat, ctx_row=ctx_row,
        casts=[ffn_w_up.reshape(depth * d, up_cols), ffn_w_down.reshape(depth * D_FF, d),
               mla_w_o[0]])
    w_up = w_up.reshape(depth, d, up_cols)
    w_down = w_down.reshape(depth, D_FF, d)
    ffn_w = [(w_up, ffn_conv_w[i], ffn_conv_b[i].reshape(1, -1), w_down) for i in range(depth)]
    xc, xl = _ffn(xc, xl, mod[0], norm_g[0], *ffn_w[0], layer=0, **groups)

    def attend(x, nb, t, mod_row, heads, rope_tables, cached):
        proj = _mla_proj(x, mod[1], norm_g[1], mla_w, tm=min(TOKEN_TILE, t), seq_len=t,
                         mod_row=mod_row, rope_tables=rope_tables)
        q, k = (a.reshape(nb, t, -1) for a in proj[:2])
        kvs = [(k, proj[2])] + ([cached] if cached is not None else [])
        return _attention(q, kvs, heads=heads).reshape(nb * t, -1), proj[3:]

    oc, (ckv, krope) = attend(xc, nb_ctx, t_ctx, lambda row: ctx_row, N_HEADS, None, None)
    kr_pad = jnp.pad(cache_krope[:, 0], ((0, 0), (0, 0), (0, ROPE_PAD - ROPE_DIM)))
    cached = _kv_expand(cache_ckv[:, 0], kr_pad, mla_w["uk"], mla_w["uv_t"])
    ol, _ = attend(xl, nb_lat, t_lat, lambda row: row // t_lat, LATENT_HEADS_PER_STEP,
                   _rope_tables(t_lat), cached)
    xc, xl = _ffn(xc, xl, mod[1], norm_g[1], *ffn_w[1], layer=1, **groups,
                  oc=oc, ol=ol, w_o=w_o)

    return (xc.reshape(nb_ctx, t_ctx, d), xl.reshape(nb_lat, t_lat, d),
            ckv.reshape(nb_ctx, 1, t_ctx, KV_RANK),
            krope.reshape(nb_ctx, 1, t_ctx, ROPE_DIM))
```

```python
import functools

import jax
import jax.numpy as jnp
import numpy as np
from jax import lax
from jax.experimental import pallas as pl
from jax.experimental.pallas import tpu as pltpu

D_MODEL = 1024
GRID_W = 64
CHUNK = 128
D_INNER = 2 * D_MODEL
N_GROUPS = 8
GROUP_DIM = D_INNER // N_GROUPS
N_HEADS = 8
NOPE_DIM = 128
ROPE_DIM = 64
V_DIM = 128
Q_RANK = 384
KV_RANK = 256
ROPE_BASE = 10000.0
D_FF = 2816
EPS = 1e-6

LANES = 128
BF16_SUBLANES = 16
MXU_DIM = 256
VMEM_LIMIT_BYTES = 56 * 1024 * 1024

HEAD_PAD = 2 * LANES
ROPE_PAD = LANES
N_COND = 16
HALO = BF16_SUBLANES
FF_TILE = MXU_DIM
TOKEN_TILE = 512
ADA_COL_TILE = 2048
LATENT_HEADS_PER_STEP = 2
MAX_DOT_ROWS = 176
VT_DOT_ROWS = 512
CAST_STEPS = 32
Q_CHUNK = MXU_DIM
KEY_CHUNK = 2 * MXU_DIM
QK_SCALE_LOG2E = float((NOPE_DIM + ROPE_DIM) ** -0.5 * np.log2(np.e))

F32 = jnp.float32
BF16 = jnp.bfloat16


def _dot(a, b):
    return jnp.dot(a, b, preferred_element_type=F32)


def _dot_nt(a, b):
    return lax.dot_general(a, b, (((1,), (1,)), ((), ())),
                           preferred_element_type=F32)


def _row_chunks(rows):
    n = -(-rows // MAX_DOT_ROWS)
    step = -(-rows // (n * BF16_SUBLANES)) * BF16_SUBLANES
    return [(lo, min(lo + step, rows)) for lo in range(0, rows, step)]


def _mdot(a, w):
    return jnp.concatenate(
        [_dot(a[lo:hi], w) for lo, hi in _row_chunks(a.shape[0])], axis=0)


def _rms(x, g):
    return x * lax.rsqrt(jnp.mean(x * x, axis=-1, keepdims=True) + EPS) * g


def _sigmoid(x):
    return 1.0 / (1.0 + jnp.exp(-x))


def _gelu(x):
    return 0.5 * x * (1.0 + lax.erf(x * np.float32(np.sqrt(0.5))))


def _params(*sem):
    return pltpu.CompilerParams(dimension_semantics=sem,
                                vmem_limit_bytes=VMEM_LIMIT_BYTES)


def _const_spec(shape, layer=None):
    if layer is None:
        block, index = shape, (0,) * len(shape)
    else:
        block, index = (None,) + tuple(shape[1:]), (layer,) + (0,) * (len(shape) - 1)
    return pl.BlockSpec(block, lambda *_: index, pipeline_mode=pl.Buffered(1))


def _ada_kernel(cond_ref, w_ref, b_ref, o_ref):
    c = cond_ref[...]
    s = (c * _sigmoid(c)).astype(BF16)
    o_ref[0] = _dot(s, w_ref[0].astype(BF16)) + b_ref[0]


def _ada(cond, ada_w, ada_b):
    depth, d, n = ada_w.shape
    tn = ADA_COL_TILE
    return pl.pallas_call(
        _ada_kernel,
        grid=(depth, n // tn),
        in_specs=[
            pl.BlockSpec((N_COND, d), lambda i, j: (0, 0)),
            pl.BlockSpec((1, d, tn), lambda i, j: (i, 0, j)),
            pl.BlockSpec((1, 1, tn), lambda i, j: (i, 0, j)),
        ],
        out_specs=pl.BlockSpec((1, N_COND, tn), lambda i, j: (i, 0, j)),
        out_shape=jax.ShapeDtypeStruct((depth, N_COND, n), F32),
        compiler_params=_params("parallel", "parallel"),
        name="ada",
    )(cond, ada_w, ada_b.reshape(depth, 1, n))


def _mixer_tile(x_ref, o_ref, mod_ref, ng_ref, w_in_ref, b_in_ref, gv_ref, ws_ref,
                bs_ref, w_out_ref, *, tm, casts):
    m = mod_ref[0]
    ng = ng_ref[...]
    x = x_ref[...]
    hb = (_rms(x, ng[0:1] * (1.0 + m[1:2])) + m[0:1]).astype(BF16)
    n_chunks = tm // CHUNK

    def pre(c):
        lo = c * GROUP_DIM
        return _mdot(hb, w_in_ref[:, lo:lo + GROUP_DIM]) + b_in_ref[:, lo:lo + GROUP_DIM]

    vs = []
    ss = None
    nxt = pre(N_GROUPS)
    for c in range(N_GROUPS):
        cur = nxt
        if c + 1 < N_GROUPS:
            nxt = pre(N_GROUPS + c + 1)
        v = _gelu(cur)
        s = jnp.sum(v * v, axis=-1, keepdims=True)
        ss = s if ss is None else ss + s
        vs.append(v)
    rs = lax.rsqrt(ss * (1.0 / D_INNER) + EPS)

    for src_ref, dst_ref in casts:
        dst_ref[...] = src_ref[...].astype(BF16)

    def mix(g):
        lo = g * GROUP_DIM
        vg = (vs[g] * rs * gv_ref[:, lo:lo + GROUP_DIM]).astype(BF16)
        w_s = ws_ref[g]
        return jnp.concatenate(
            [_dot(w_s, vg[n * CHUNK:(n + 1) * CHUNK]) for n in range(n_chunks)],
            axis=0)

    def down(t, g, acc):
        d = _mdot(t, w_out_ref[g * GROUP_DIM:(g + 1) * GROUP_DIM, :])
        return d if acc is None else acc + d

    acc = None
    t_prev = None
    nxt = (pre(0), mix(0))
    for g in range(N_GROUPS):
        u_pre, mixed = nxt
        if g + 1 < N_GROUPS:
            nxt = (pre(g + 1), mix(g + 1))
        if t_prev is not None:
            acc = down(t_prev, g - 1, acc)
        b_s = jnp.concatenate([bs_ref[:, g:g + 1]] * n_chunks, axis=0)
        t_prev = (_gelu(u_pre) * (mixed + b_s)).astype(BF16)
    acc = down(t_prev, N_GROUPS - 1, acc)
    o_ref[...] = x + _rms(acc, m[2:3] * ng[1:2])


def _mixer_kernel(xc_ref, xl_ref, *refs, tm, n_ctx, n_cast):
    ins, (oc_ref, ol_ref, *cast_out) = refs[:-(2 + n_cast)], refs[-(2 + n_cast):]
    shared, cast_in = ins[:len(ins) - n_cast], ins[len(ins) - n_cast:]
    tile = functools.partial(_mixer_tile, tm=tm, casts=tuple(zip(cast_in, cast_out)))
    i = pl.program_id(0)
    pl.when(i < n_ctx)(functools.partial(tile, xc_ref, oc_ref, *shared))
    pl.when(i >= n_ctx)(functools.partial(tile, xl_ref, ol_ref, *shared))


class _TwoGroups:
    def __init__(self, n_ctx, n_lat, tm_lat, t_lat, ctx_row):
        self.n_ctx, self.n_lat = n_ctx, n_lat
        self.tm_lat, self.t_lat, self.ctx_row = tm_lat, t_lat, ctx_row

    def ctx(self, i):
        return jnp.minimum(i, self.n_ctx - 1)

    def lat(self, i):
        return jnp.maximum(i - self.n_ctx, 0)

    def mod_spec(self, d):
        def index(i):
            lat_row = self.lat(i) * self.tm_lat // self.t_lat
            return (jnp.where(i < self.n_ctx, self.ctx_row, lat_row), 0, 0)
        return pl.BlockSpec((1, 6, d), index)


def _mixer(xc, xl, mod, ng, w_in, b_in, g_v, w_s, b_s_t, w_out, *, tm, t_lat, ctx_row,
           casts):
    d = xc.shape[1]
    g = _TwoGroups(xc.shape[0] // tm, xl.shape[0] // tm, tm, t_lat, ctx_row)
    n_steps = g.n_ctx + g.n_lat
    ctx_spec = pl.BlockSpec((tm, d), lambda i: (g.ctx(i), 0))
    lat_spec = pl.BlockSpec((tm, d), lambda i: (g.lat(i), 0))
    consts = [ng, w_in, b_in, g_v, w_s, b_s_t, w_out]

    def cast_spec(a):
        rows, cols = a.shape
        assert rows % (CAST_STEPS * BF16_SUBLANES) == 0 and CAST_STEPS <= n_steps
        return pl.BlockSpec((rows // CAST_STEPS, cols),
                            lambda i: (jnp.minimum(i, CAST_STEPS - 1), 0))

    cast_specs = [cast_spec(a) for a in casts]
    return pl.pallas_call(
        functools.partial(_mixer_kernel, tm=tm, n_ctx=g.n_ctx, n_cast=len(casts)),
        grid=(n_steps,),
        in_specs=([ctx_spec, lat_spec, g.mod_spec(d)]
                  + [_const_spec(a.shape) for a in consts] + cast_specs),
        out_specs=[ctx_spec, lat_spec] + cast_specs,
        out_shape=([jax.ShapeDtypeStruct(xc.shape, F32), jax.ShapeDtypeStruct(xl.shape, F32)]
                   + [jax.ShapeDtypeStruct(a.shape, BF16) for a in casts]),
        compiler_params=_params("arbitrary"),
        name="mixer",
    )(xc, xl, mod, *consts, *casts)


def _ffn_tile(x_refs, o_refs, out_ref, tile, shared, *, tm, seq_len):
    (wo_ref, mod_ref, ng_ref, wup_ref, cw_ref, cb_ref, wd_ref, hn_scr, x1_scr) = shared
    fuse_oproj = o_refs is not None
    x_ref, xp_ref, xn_ref = x_refs
    o_ref, op_ref, on_ref = o_refs if fuse_oproj else (None, None, None)
    m = mod_ref[0]
    ng = ng_ref[...]
    n_stages = D_FF // FF_TILE
    rows = tm + HALO

    x1_all = jnp.concatenate([xp_ref[...], x_ref[...], xn_ref[...]], axis=0)
    if fuse_oproj:
        o_all = jnp.concatenate([op_ref[...], o_ref[...], on_ref[...]], axis=0)
        x1_all = x1_all + _rms(_mdot(o_all, wo_ref[...]), m[2:3] * ng[1:2])
    hn_all = _rms(x1_all, ng[2:3] * (1.0 + m[4:5])) + m[3:4]
    x1_scr[0:tm] = x1_all[HALO:HALO + tm]
    hn_scr[HALO:rows] = hn_all[HALO:HALO + tm].astype(BF16)
    row0 = tile * tm
    at_start = (row0 % seq_len) == 0
    at_end = ((row0 + tm) % seq_len) == 0
    hp = hn_all[:HALO]
    hp = jnp.where(at_start, jnp.zeros_like(hp), hp)
    hx = hn_all[HALO + tm:]
    hx = jnp.where(at_end, jnp.zeros_like(hx), hx)
    rid = lax.broadcasted_iota(jnp.int32, hp.shape, 0)
    hn_scr[0:HALO] = jnp.where(rid == 0, hx, hp).astype(BF16)
    hn = hn_scr[0:rows]

    def up_half(j, half):
        lo = half * D_FF + j * FF_TILE
        return _mdot(hn, wup_ref[:, lo:lo + FF_TILE])

    def conv(h, lo):
        cw = cw_ref[:, lo:lo + FF_TILE]
        return (pltpu.roll(h, 1, axis=0)[HALO:] * cw[0:1] + h[HALO:] * cw[1:2]
                + pltpu.roll(h, rows - 1, axis=0)[HALO:] * cw[2:3]
                + cb_ref[:, lo:lo + FF_TILE])

    def down(t, j, acc):
        d = _mdot(t, wd_ref[j * FF_TILE:(j + 1) * FF_TILE, :])
        return d if acc is None else acc + d

    nxt = (up_half(0, 0), up_half(0, 1))
    acc = None
    t_prev = None
    for j in range(n_stages):
        hg, hv = nxt
        more = j + 1 < n_stages
        next_g = up_half(j + 1, 0) if more else None
        if t_prev is not None:
            acc = down(t_prev, j - 1, acc)
        cg = conv(hg, j * FF_TILE)
        next_v = up_half(j + 1, 1) if more else None
        cv = conv(hv, D_FF + j * FF_TILE)
        t_prev = (cg * _sigmoid(cg) * cv).astype(BF16)
        nxt = (next_g, next_v)
    acc = down(t_prev, n_stages - 1, acc)
    out_ref[...] = x1_scr[0:tm] + _rms(acc, m[5:6] * ng[3:4])


def _ffn_kernel(*refs, n_ctx, tm_ctx, t_ctx, tm_lat, t_lat, fuse_oproj):
    per_group = 6 if fuse_oproj else 3
    ctx_in, lat_in = refs[:per_group], refs[per_group:2 * per_group]
    refs = refs[2 * per_group:]
    if not fuse_oproj:
        refs = (None,) + refs
    *weights, outc_ref, outl_ref, hn_scr, x1_scr = refs
    shared = (*weights, hn_scr, x1_scr)
    i = pl.program_id(0)

    def run(group_in, out_ref, tile, tm, seq_len):
        o_refs = group_in[3:] if fuse_oproj else None
        _ffn_tile(group_in[:3], o_refs, out_ref, tile, shared, tm=tm, seq_len=seq_len)

    pl.when(i < n_ctx)(
        functools.partial(run, ctx_in, outc_ref, i, tm_ctx, t_ctx))
    pl.when(i >= n_ctx)(
        functools.partial(run, lat_in, outl_ref, i - n_ctx, tm_lat, t_lat))


def _ffn(xc, xl, mod, ng, w_up, conv_w, conv_b, w_down, *, layer, t_ctx, t_lat, ctx_row,
         oc=None, ol=None, w_o=None):
    d = xc.shape[1]
    fuse = oc is not None
    tm_ctx, tm_lat = min(TOKEN_TILE, t_ctx), min(TOKEN_TILE, t_lat)
    g = _TwoGroups(xc.shape[0] // tm_ctx, xl.shape[0] // tm_lat, tm_lat, t_lat, ctx_row)

    def group_specs(n_rows, tm, tile):
        hb = tm // HALO
        last = n_rows // HALO - 1
        return [pl.BlockSpec((tm, d), lambda i: (tile(i), 0)),
                pl.BlockSpec((HALO, d), lambda i: (jnp.maximum(tile(i) * hb - 1, 0), 0)),
                pl.BlockSpec((HALO, d), lambda i: (jnp.minimum((tile(i) + 1) * hb, last), 0))]

    args, specs = [], []
    for x, o, tm, tile in ((xc, oc, tm_ctx, g.ctx), (xl, ol, tm_lat, g.lat)):
        for a in ([x, o] if fuse else [x]):
            args += [a, a, a]
            specs += group_specs(x.shape[0], tm, tile)
    consts = ([w_o] if fuse else []) + [mod, ng, w_up, conv_w, conv_b, w_down]
    args += consts
    for a in consts:
        if a is mod:
            specs.append(g.mod_spec(d))
        else:
            specs.append(_const_spec(a.shape, layer if a is w_up or a is w_down else None))
    return pl.pallas_call(
        functools.partial(_ffn_kernel, n_ctx=g.n_ctx, tm_ctx=tm_ctx, t_ctx=t_ctx,
                          tm_lat=tm_lat, t_lat=t_lat, fuse_oproj=fuse),
        grid=(g.n_ctx + g.n_lat,),
        in_specs=specs,
        out_specs=[pl.BlockSpec((tm_ctx, d), lambda i: (g.ctx(i), 0)),
                   pl.BlockSpec((tm_lat, d), lambda i: (g.lat(i), 0))],
        out_shape=[jax.ShapeDtypeStruct(xc.shape, F32), jax.ShapeDtypeStruct(xl.shape, F32)],
        scratch_shapes=[pltpu.VMEM((max(tm_ctx, tm_lat) + HALO, d), BF16),
                        pltpu.VMEM((max(tm_ctx, tm_lat), d), F32)],
        compiler_params=_params("arbitrary"),
        name="ffn_oproj" if fuse else "ffn",
    )(*args)


def _store_keys(k_ref, kn, krb):
    for h in range(N_HEADS):
        lo = h * HEAD_PAD
        k_ref[:, lo:lo + NOPE_DIM] = kn[:, h * NOPE_DIM:(h + 1) * NOPE_DIM].astype(BF16)
        k_ref[:, lo + NOPE_DIM:lo + HEAD_PAD] = krb


def _store_values_t(vt_ref, wuvt_ref, ckvb):
    for lo in range(0, N_HEADS * V_DIM, VT_DOT_ROWS):
        vt_ref[0, lo:lo + VT_DOT_ROWS, :] = _dot_nt(
            wuvt_ref[lo:lo + VT_DOT_ROWS, :], ckvb).astype(BF16)


def _mla_proj_kernel(*refs, rope):
    (x_ref, mod_ref, ng_ref, wdown_ref, gkv_ref, gq_ref, wuq_ref,
     wuk_ref, wuvt_ref, *refs) = refs
    if rope:
        cos_ref, sin_ref, q_ref, k_ref, vt_ref = refs
        cos = cos_ref[...]
        sin = sin_ref[...]
    else:
        q_ref, k_ref, vt_ref, ckv_ref, kr_ref = refs
    m = mod_ref[0]
    ng = ng_ref[...]
    hb = (_rms(x_ref[...], ng[0:1] * (1.0 + m[1:2])) + m[0:1]).astype(BF16)
    down = _mdot(hb, wdown_ref[...])
    ckv = _rms(down[:, :KV_RANK], gkv_ref[...])
    kr = down[:, KV_RANK:KV_RANK + ROPE_PAD]
    if rope:
        kr = kr * cos + pltpu.roll(kr, ROPE_DIM, axis=1) * sin
    else:
        ckv_ref[...] = ckv
        kr_ref[...] = kr[:, :ROPE_DIM]
    ql = _rms(down[:, KV_RANK + ROPE_PAD:], gq_ref[...]).astype(BF16)
    q = _mdot(ql, wuq_ref[...])
    ckvb = ckv.astype(BF16)
    _store_values_t(vt_ref, wuvt_ref, ckvb)
    _store_keys(k_ref, _mdot(ckvb, wuk_ref[...]), kr.astype(BF16))
    for h in range(N_HEADS):
        lo = h * HEAD_PAD
        mid = lo + NOPE_DIM
        q_ref[:, lo:mid] = (q[:, lo:mid] * QK_SCALE_LOG2E).astype(BF16)
        qr = q[:, mid:lo + HEAD_PAD]
        if rope:
            qr = qr * cos + pltpu.roll(qr, ROPE_DIM, axis=1) * sin
        q_ref[:, mid:lo + HEAD_PAD] = (qr * QK_SCALE_LOG2E).astype(BF16)


def _mla_proj(x, mod, ng, w, *, tm, seq_len, mod_row, rope_tables=None):
    n, d = x.shape
    rope = rope_tables is not None
    tiles_per_seq = seq_len // tm
    consts = [ng, w["down_rope"] if rope else w["down"], w["g_kv"], w["g_q"],
              w["uq_rope"] if rope else w["uq"], w["uk"], w["uv_t"]]
    args = [x, mod] + consts
    specs = [pl.BlockSpec((tm, d), lambda i: (i, 0)),
             pl.BlockSpec((1, 6, d), lambda i: (mod_row(i * tm), 0, 0))]
    specs += [_const_spec(a.shape) for a in consts]
    row = lambda i: (i, 0)
    col = lambda i: (i // tiles_per_seq, 0, i % tiles_per_seq)
    out_shape = [jax.ShapeDtypeStruct((n, N_HEADS * HEAD_PAD), BF16),
                 jax.ShapeDtypeStruct((n, N_HEADS * HEAD_PAD), BF16),
                 jax.ShapeDtypeStruct((n // seq_len, N_HEADS * V_DIM, seq_len), BF16)]
    out_specs = [pl.BlockSpec((tm, N_HEADS * HEAD_PAD), row),
                 pl.BlockSpec((tm, N_HEADS * HEAD_PAD), row),
                 pl.BlockSpec((1, N_HEADS * V_DIM, tm), col)]
    if rope:
        args += list(rope_tables)
        specs += [pl.BlockSpec((tm, ROPE_PAD), lambda i: (i % tiles_per_seq, 0))] * 2
    else:
        out_shape += [jax.ShapeDtypeStruct((n, KV_RANK), F32),
                      jax.ShapeDtypeStruct((n, ROPE_DIM), F32)]
        out_specs += [pl.BlockSpec((tm, KV_RANK), row),
                      pl.BlockSpec((tm, ROPE_DIM), row)]
    return pl.pallas_call(
        functools.partial(_mla_proj_kernel, rope=rope),
        grid=(n // tm,),
        in_specs=specs,
        out_specs=out_specs,
        out_shape=out_shape,
        compiler_params=_params("parallel"),
        name="mla_proj_rope" if rope else "mla_proj",
    )(*args)


def _kv_expand_kernel(ckv_ref, kr_ref, wuk_ref, wuvt_ref, k_ref, vt_ref):
    ckvb = ckv_ref[0].astype(BF16)
    _store_values_t(vt_ref, wuvt_ref, ckvb)
    _store_keys(k_ref.at[0], _mdot(ckvb, wuk_ref[...]), kr_ref[0].astype(BF16))


def _kv_expand(ckv, kr_pad, w_uk, w_uv_t):
    b, s, _ = ckv.shape
    blk = lambda i: (i, 0, 0)
    return pl.pallas_call(
        _kv_expand_kernel,
        grid=(b,),
        in_specs=[pl.BlockSpec((1, s, KV_RANK), blk),
                  pl.BlockSpec((1, s, ROPE_PAD), blk),
                  _const_spec(w_uk.shape), _const_spec(w_uv_t.shape)],
        out_specs=[pl.BlockSpec((1, s, N_HEADS * HEAD_PAD), blk),
                   pl.BlockSpec((1, N_HEADS * V_DIM, s), blk)],
        out_shape=[jax.ShapeDtypeStruct((b, s, N_HEADS * HEAD_PAD), BF16),
                   jax.ShapeDtypeStruct((b, N_HEADS * V_DIM, s), BF16)],
        compiler_params=_params("parallel"),
        name="kv_expand",
    )(ckv, kr_pad, w_uk, w_uv_t)


def _attn_kernel(*refs, n_kv, heads, t):
    q_ref = refs[0]
    kv_refs = refs[1:1 + 2 * n_kv]
    o_ref = refs[1 + 2 * n_kv]
    items = [(h, c) for h in range(heads) for c in range(t // Q_CHUNK)]

    def key_blocks(s):
        n_keys = kv_refs[2 * s].shape[1]
        step = min(KEY_CHUNK, n_keys)
        return [(r, r + step) for r in range(0, n_keys, step)]

    def scores(item):
        h, c = item
        q = q_ref[0, c * Q_CHUNK:(c + 1) * Q_CHUNK, h * HEAD_PAD:(h + 1) * HEAD_PAD]
        return [_dot_nt(kv_refs[2 * s][0, lo:hi, h * HEAD_PAD:(h + 1) * HEAD_PAD], q)
                for s in range(n_kv) for lo, hi in key_blocks(s)]

    def softmax_pv(item, blocks):
        h, c = item
        mx = functools.reduce(
            jnp.maximum, [jnp.max(b, axis=0, keepdims=True) for b in blocks])
        ps = [jnp.exp2(b - mx) for b in blocks]
        den = functools.reduce(
            jnp.add, [jnp.sum(p, axis=0, keepdims=True) for p in ps])
        pt = jnp.concatenate([p.astype(BF16) for p in ps], axis=0)
        vt = jnp.concatenate([kv_refs[2 * s + 1][0, h * V_DIM:(h + 1) * V_DIM, :]
                              for s in range(n_kv)], axis=1)
        o = (_dot(vt, pt) * (1.0 / den)).T
        o_ref[0, c * Q_CHUNK:(c + 1) * Q_CHUNK, h * V_DIM:(h + 1) * V_DIM] = o.astype(BF16)

    nxt = scores(items[0])
    for n, item in enumerate(items):
        cur = nxt
        if n + 1 < len(items):
            nxt = scores(items[n + 1])
        softmax_pv(item, cur)


def _attention(q, kvs, *, heads):
    b, t, _ = q.shape
    args = [q]
    specs = [pl.BlockSpec((1, t, heads * HEAD_PAD), lambda bi, h: (bi, 0, h))]
    for k, vt in kvs:
        s = k.shape[1]
        args += [k, vt]
        specs += [pl.BlockSpec((1, s, heads * HEAD_PAD), lambda bi, h: (bi, 0, h)),
                  pl.BlockSpec((1, heads * V_DIM, s), lambda bi, h: (bi, h, 0))]
    return pl.pallas_call(
        functools.partial(_attn_kernel, n_kv=len(kvs), heads=heads, t=t),
        grid=(b, N_HEADS // heads),
        in_specs=specs,
        out_specs=pl.BlockSpec((1, t, heads * V_DIM), lambda bi, h: (bi, 0, h)),
        out_shape=jax.ShapeDtypeStruct((b, t, N_HEADS * V_DIM), BF16),
        compiler_params=_params("parallel", "parallel"),
        name="attention",
    )(*args)


_ROPE_SWAP = np.concatenate([np.arange(16, 32), np.arange(0, 16),
                             np.arange(48, 64), np.arange(32, 48)])


def _mla_weights(w_dq, g_q, w_uq, w_dkv, g_kv, w_uk, w_uv):
    zpad = jnp.zeros((D_MODEL, ROPE_PAD - ROPE_DIM), F32)
    k_rope = w_dkv[:, KV_RANK:]
    down = jnp.concatenate([w_dkv, zpad, w_dq], axis=1)
    down_rope = jnp.concatenate([w_dkv, k_rope[:, _ROPE_SWAP], w_dq], axis=1)
    uq = w_uq.reshape(Q_RANK, N_HEADS, NOPE_DIM + ROPE_DIM)
    q_rope = uq[:, :, NOPE_DIM:]
    hpad = jnp.zeros((Q_RANK, N_HEADS, ROPE_PAD - ROPE_DIM), F32)
    uq_full = jnp.concatenate([uq, hpad], axis=2).reshape(Q_RANK, N_HEADS * HEAD_PAD)
    uq_rope = jnp.concatenate([uq, q_rope[:, :, _ROPE_SWAP]], axis=2)
    uq_rope = uq_rope.reshape(Q_RANK, N_HEADS * HEAD_PAD)
    return dict(down=down.astype(BF16), down_rope=down_rope.astype(BF16),
                g_kv=g_kv.reshape(1, -1), g_q=g_q.reshape(1, -1),
                uq=uq_full.astype(BF16), uq_rope=uq_rope.astype(BF16),
                uk=w_uk.astype(BF16), uv_t=w_uv.T.astype(BF16))


def _rope_tables(t):
    half = ROPE_DIM // 4
    pos = np.arange(t)
    inv = ROPE_BASE ** (-np.arange(half, dtype=np.float64) / half)
    ang_r = (pos // GRID_W)[:, None] * inv[None, :]
    ang_c = (pos % GRID_W)[:, None] * inv[None, :]
    zeros = np.zeros((t, ROPE_PAD - ROPE_DIM))
    cos = np.concatenate([np.cos(ang_r), np.cos(ang_r), np.cos(ang_c),
                          np.cos(ang_c), zeros], axis=1)
    sin = np.concatenate([-np.sin(ang_r), np.sin(ang_r), -np.sin(ang_c),
                          np.sin(ang_c), zeros], axis=1)
    return jnp.asarray(cos, F32), jnp.asarray(sin, F32)


def kernel(x_prompt, x_sample, cache_ckv, cache_krope, c, c_ctx, ada_w, ada_b, norm_g, gm_w_in, gm_b_in, gm_g_v, gm_w_s, gm_b_s, gm_w_out, mla_w_dq, mla_g_q, mla_w_uq, mla_w_dkv, mla_g_kv, mla_w_uk, mla_w_uv, mla_w_o, ffn_w_up, ffn_conv_w, ffn_conv_b, ffn_w_down):
    nb_ctx, t_ctx, d = x_prompt.shape
    nb_lat, t_lat, _ = x_sample.shape
    ctx_row = nb_lat

    cond = jnp.concatenate(
        [c, c_ctx[None, :], jnp.zeros((N_COND - nb_lat - 1, d), F32)], axis=0)
    mod = _ada(cond, ada_w, ada_b).reshape(ada_w.shape[0], N_COND, 6, d)

    mixer_w = (gm_w_in[0].astype(BF16), gm_b_in[0].reshape(1, -1),
               gm_g_v[0].reshape(1, -1), gm_w_s[0].astype(BF16), gm_b_s[0].T,
               gm_w_out[0].astype(BF16))
    mla_w = _mla_weights(mla_w_dq[0], mla_g_q[0], mla_w_uq[0], mla_w_dkv[0],
                         mla_g_kv[0], mla_w_uk[0], mla_w_uv[0])

    groups = dict(t_ctx=t_ctx, t_lat=t_lat, ctx_row=ctx_row)
    xc = x_prompt.reshape(nb_ctx * t_ctx, d)
    xl = x_sample.reshape(nb_lat * t_lat, d)
    depth, _, up_cols = ffn_w_up.shape
    xc, xl, w_up, w_down, w_o = _mixer(
        xc, xl, mod[0], norm_g[0], *mixer_w, tm=TOKEN_TILE, t_lat=t_lat, ctx_row=ctx_row,
        casts=[ffn_w_up.reshape(depth * d, up_cols), ffn_w_down.reshape(depth * D_FF, d),
               mla_w_o[0]])
    w_up = w_up.reshape(depth, d, up_cols)
    w_down = w_down.reshape(depth, D_FF, d)
    ffn_w = [(w_up, ffn_conv_w[i], ffn_conv_b[i].reshape(1, -1), w_down) for i in range(depth)]
    xc, xl = _ffn(xc, xl, mod[0], norm_g[0], *ffn_w[0], layer=0, **groups)

    def attend(x, nb, t, mod_row, heads, rope_tables, cached):
        proj = _mla_proj(x, mod[1], norm_g[1], mla_w, tm=min(TOKEN_TILE, t), seq_len=t,
                         mod_row=mod_row, rope_tables=rope_tables)
        q, k = (a.reshape(nb, t, -1) for a in proj[:2])
        kvs = [(k, proj[2])] + ([cached] if cached is not None else [])
        return _attention(q, kvs, heads=heads).reshape(nb * t, -1), proj[3:]

    oc, (ckv, krope) = attend(xc, nb_ctx, t_ctx, lambda row: ctx_row, N_HEADS, None, None)
    kr_pad = jnp.pad(cache_krope[:, 0], ((0, 0), (0, 0), (0, ROPE_PAD - ROPE_DIM)))
    cached = _kv_expand(cache_ckv[:, 0], kr_pad, mla_w["uk"], mla_w["uv_t"])
    ol, _ = attend(xl, nb_lat, t_lat, lambda row: row // t_lat, LATENT_HEADS_PER_STEP,
                   _rope_tables(t_lat), cached)
    xc, xl = _ffn(xc, xl, mod[1], norm_g[1], *ffn_w[1], layer=1, **groups,
                  oc=oc, ol=ol, w_o=w_o)

    return (xc.reshape(nb_ctx, t_ctx, d), xl.reshape(nb_lat, t_lat, d),
            ckv.reshape(nb_ctx, 1, t_ctx, KV_RANK),
            krope.reshape(nb_ctx, 1, t_ctx, ROPE_DIM))
```

```python
import functools

import jax
import jax.numpy as jnp
import numpy as np
from jax import lax
from jax.experimental import pallas as pl
from jax.experimental.pallas import tpu as pltpu

D_MODEL = 1024
GRID_W = 64
CHUNK = 128
D_INNER = 2 * D_MODEL
N_GROUPS = 8
GROUP_DIM = D_INNER // N_GROUPS
N_HEADS = 8
NOPE_DIM = 128
ROPE_DIM = 64
V_DIM = 128
Q_RANK = 384
KV_RANK = 256
ROPE_BASE = 10000.0
D_FF = 2816
EPS = 1e-6

LANES = 128
BF16_SUBLANES = 16
MXU_DIM = 256
VMEM_LIMIT_BYTES = 56 * 1024 * 1024

HEAD_PAD = 2 * LANES
ROPE_PAD = LANES
N_COND = 16
HALO = BF16_SUBLANES
FF_TILE = MXU_DIM
TOKEN_TILE = 512
ADA_COL_TILE = 2048
LATENT_HEADS_PER_STEP = 2
MAX_DOT_ROWS = 176
VT_DOT_ROWS = 512
CAST_STEPS = 32
Q_CHUNK = MXU_DIM
KEY_CHUNK = 2 * MXU_DIM
QK_SCALE_LOG2E = float((NOPE_DIM + ROPE_DIM) ** -0.5 * np.log2(np.e))

F32 = jnp.float32
BF16 = jnp.bfloat16


def _dot(a, b):
    return jnp.dot(a, b, preferred_element_type=F32)


def _dot_nt(a, b):
    return lax.dot_general(a, b, (((1,), (1,)), ((), ())),
                           preferred_element_type=F32)


def _row_chunks(rows):
    n = -(-rows // MAX_DOT_ROWS)
    step = -(-rows // (n * BF16_SUBLANES)) * BF16_SUBLANES
    return [(lo, min(lo + step, rows)) for lo in range(0, rows, step)]


def _mdot(a, w):
    return jnp.concatenate(
        [_dot(a[lo:hi], w) for lo, hi in _row_chunks(a.shape[0])], axis=0)


def _rms(x, g):
    return x * lax.rsqrt(jnp.mean(x * x, axis=-1, keepdims=True) + EPS) * g


def _sigmoid(x):
    return 1.0 / (1.0 + jnp.exp(-x))


def _gelu(x):
    return 0.5 * x * (1.0 + lax.erf(x * np.float32(np.sqrt(0.5))))


def _params(*sem):
    return pltpu.CompilerParams(dimension_semantics=sem,
                                vmem_limit_bytes=VMEM_LIMIT_BYTES)


def _const_spec(shape, layer=None):
    if layer is None:
        block, index = shape, (0,) * len(shape)
    else:
        block, index = (None,) + tuple(shape[1:]), (layer,) + (0,) * (len(shape) - 1)
    return pl.BlockSpec(block, lambda *_: index, pipeline_mode=pl.Buffered(1))


def _ada_kernel(cond_ref, w_ref, b_ref, o_ref):
    c = cond_ref[...]
    s = (c * _sigmoid(c)).astype(BF16)
    o_ref[0] = _dot(s, w_ref[0].astype(BF16)) + b_ref[0]


def _ada(cond, ada_w, ada_b):
    depth, d, n = ada_w.shape
    tn = ADA_COL_TILE
    return pl.pallas_call(
        _ada_kernel,
        grid=(depth, n // tn),
        in_specs=[
            pl.BlockSpec((N_COND, d), lambda i, j: (0, 0)),
            pl.BlockSpec((1, d, tn), lambda i, j: (i, 0, j)),
            pl.BlockSpec((1, 1, tn), lambda i, j: (i, 0, j)),
        ],
        out_specs=pl.BlockSpec((1, N_COND, tn), lambda i, j: (i, 0, j)),
        out_shape=jax.ShapeDtypeStruct((depth, N_COND, n), F32),
        compiler_params=_params("parallel", "parallel"),
        name="ada",
    )(cond, ada_w, ada_b.reshape(depth, 1, n))


def _mixer_tile(x_ref, o_ref, mod_ref, ng_ref, w_in_ref, b_in_ref, gv_ref, ws_ref,
                bs_ref, w_out_ref, *, tm, casts):
    m = mod_ref[0]
    ng = ng_ref[...]
    x = x_ref[...]
    hb = (_rms(x, ng[0:1] * (1.0 + m[1:2])) + m[0:1]).astype(BF16)
    n_chunks = tm // CHUNK

    def pre(c):
        lo = c * GROUP_DIM
        return _mdot(hb, w_in_ref[:, lo:lo + GROUP_DIM]) + b_in_ref[:, lo:lo + GROUP_DIM]

    vs = []
    ss = None
    nxt = pre(N_GROUPS)
    for c in range(N_GROUPS):
        cur = nxt
        if c + 1 < N_GROUPS:
            nxt = pre(N_GROUPS + c + 1)
        v = _gelu(cur)
        s = jnp.sum(v * v, axis=-1, keepdims=True)
        ss = s if ss is None else ss + s
        vs.append(v)
    rs = lax.rsqrt(ss * (1.0 / D_INNER) + EPS)

    for src_ref, dst_ref in casts:
        dst_ref[...] = src_ref[...].astype(BF16)

    def mix(g):
        lo = g * GROUP_DIM
        vg = (vs[g] * rs * gv_ref[:, lo:lo + GROUP_DIM]).astype(BF16)
        w_s = ws_ref[g]
        return jnp.concatenate(
            [_dot(w_s, vg[n * CHUNK:(n + 1) * CHUNK]) for n in range(n_chunks)],
            axis=0)

    def down(t, g, acc):
        d = _mdot(t, w_out_ref[g * GROUP_DIM:(g + 1) * GROUP_DIM, :])
        return d if acc is None else acc + d

    acc = None
    t_prev = None
    nxt = (pre(0), mix(0))
    for g in range(N_GROUPS):
        u_pre, mixed = nxt
        if g + 1 < N_GROUPS:
            nxt = (pre(g + 1), mix(g + 1))
        if t_prev is not None:
            acc = down(t_prev, g - 1, acc)
        b_s = jnp.concatenate([bs_ref[:, g:g + 1]] * n_chunks, axis=0)
        t_prev = (_gelu(u_pre) * (mixed + b_s)).astype(BF16)
    acc = down(t_prev, N_GROUPS - 1, acc)
    o_ref[...] = x + _rms(acc, m[2:3] * ng[1:2])


def _mixer_kernel(xc_ref, xl_ref, *refs, tm, n_ctx, n_cast):
    ins, (oc_ref, ol_ref, *cast_out) = refs[:-(2 + n_cast)], refs[-(2 + n_cast):]
    shared, cast_in = ins[:len(ins) - n_cast], ins[len(ins) - n_cast:]
    tile = functools.partial(_mixer_tile, tm=tm, casts=tuple(zip(cast_in, cast_out)))
    i = pl.program_id(0)
    pl.when(i < n_ctx)(functools.partial(tile, xc_ref, oc_ref, *shared))
    pl.when(i >= n_ctx)(functools.partial(tile, xl_ref, ol_ref, *shared))


class _TwoGroups:
    def __init__(self, n_ctx, n_lat, tm_lat, t_lat, ctx_row):
        self.n_ctx, self.n_lat = n_ctx, n_lat
        self.tm_lat, self.t_lat, self.ctx_row = tm_lat, t_lat, ctx_row

    def ctx(self, i):
        return jnp.minimum(i, self.n_ctx - 1)

    def lat(self, i):
        return jnp.maximum(i - self.n_ctx, 0)

    def mod_spec(self, d):
        def index(i):
            lat_row = self.lat(i) * self.tm_lat // self.t_lat
            return (jnp.where(i < self.n_ctx, self.ctx_row, lat_row), 0, 0)
        return pl.BlockSpec((1, 6, d), index)


def _mixer(xc, xl, mod, ng, w_in, b_in, g_v, w_s, b_s_t, w_out, *, tm, t_lat, ctx_row,
           casts):
    d = xc.shape[1]
    g = _TwoGroups(xc.shape[0] // tm, xl.shape[0] // tm, tm, t_lat, ctx_row)
    n_steps = g.n_ctx + g.n_lat
    ctx_spec = pl.BlockSpec((tm, d), lambda i: (g.ctx(i), 0))
    lat_spec = pl.BlockSpec((tm, d), lambda i: (g.lat(i), 0))
    consts = [ng, w_in, b_in, g_v, w_s, b_s_t, w_out]

    def cast_spec(a):
        rows, cols = a.shape
        assert rows % (CAST_STEPS * BF16_SUBLANES) == 0 and CAST_STEPS <= n_steps
        return pl.BlockSpec((rows // CAST_STEPS, cols),
                            lambda i: (jnp.minimum(i, CAST_STEPS - 1), 0))

    cast_specs = [cast_spec(a) for a in casts]
    return pl.pallas_call(
        functools.partial(_mixer_kernel, tm=tm, n_ctx=g.n_ctx, n_cast=len(casts)),
        grid=(n_steps,),
        in_specs=([ctx_spec, lat_spec, g.mod_spec(d)]
                  + [_const_spec(a.shape) for a in consts] + cast_specs),
        out_specs=[ctx_spec, lat_spec] + cast_specs,
        out_shape=([jax.ShapeDtypeStruct(xc.shape, F32), jax.ShapeDtypeStruct(xl.shape, F32)]
                   + [jax.ShapeDtypeStruct(a.shape, BF16) for a in casts]),
        compiler_params=_params("arbitrary"),
        name="mixer",
    )(xc, xl, mod, *consts, *casts)


def _ffn_tile(x_refs, o_refs, out_ref, tile, shared, *, tm, seq_len):
    (wo_ref, mod_ref, ng_ref, wup_ref, cw_ref, cb_ref, wd_ref, hn_scr, x1_scr) = shared
    fuse_oproj = o_refs is not None
    x_ref, xp_ref, xn_ref = x_refs
    o_ref, op_ref, on_ref = o_refs if fuse_oproj else (None, None, None)
    m = mod_ref[0]
    ng = ng_ref[...]
    n_stages = D_FF // FF_TILE
    rows = tm + HALO

    x1_all = jnp.concatenate([xp_ref[...], x_ref[...], xn_ref[...]], axis=0)
    if fuse_oproj:
        o_all = jnp.concatenate([op_ref[...], o_ref[...], on_ref[...]], axis=0)
        x1_all = x1_all + _rms(_mdot(o_all, wo_ref[...]), m[2:3] * ng[1:2])
    hn_all = _rms(x1_all, ng[2:3] * (1.0 + m[4:5])) + m[3:4]
    x1_scr[0:tm] = x1_all[HALO:HALO + tm]
    hn_scr[HALO:rows] = hn_all[HALO:HALO + tm].astype(BF16)
    row0 = tile * tm
    at_start = (row0 % seq_len) == 0
    at_end = ((row0 + tm) % seq_len) == 0
    hp = hn_all[:HALO]
    hp = jnp.where(at_start, jnp.zeros_like(hp), hp)
    hx = hn_all[HALO + tm:]
    hx = jnp.where(at_end, jnp.zeros_like(hx), hx)
    rid = lax.broadcasted_iota(jnp.int32, hp.shape, 0)
    hn_scr[0:HALO] = jnp.where(rid == 0, hx, hp).astype(BF16)
    hn = hn_scr[0:rows]

    def up_half(j, half):
        lo = half * D_FF + j * FF_TILE
        return _mdot(hn, wup_ref[:, lo:lo + FF_TILE])

    def conv(h, lo):
        cw = cw_ref[:, lo:lo + FF_TILE]
        return (pltpu.roll(h, 1, axis=0)[HALO:] * cw[0:1] + h[HALO:] * cw[1:2]
                + pltpu.roll(h, rows - 1, axis=0)[HALO:] * cw[2:3]
                + cb_ref[:, lo:lo + FF_TILE])

    def down(t, j, acc):
        d = _mdot(t, wd_ref[j * FF_TILE:(j + 1) * FF_TILE, :])
        return d if acc is None else acc + d

    nxt = (up_half(0, 0), up_half(0, 1))
    acc = None
    t_prev = None
    for j in range(n_stages):
        hg, hv = nxt
        more = j + 1 < n_stages
        next_g = up_half(j + 1, 0) if more else None
        if t_prev is not None:
            acc = down(t_prev, j - 1, acc)
        cg = conv(hg, j * FF_TILE)
        next_v = up_half(j + 1, 1) if more else None
        cv = conv(hv, D_FF + j * FF_TILE)
        t_prev = (cg * _sigmoid(cg) * cv).astype(BF16)
        nxt = (next_g, next_v)
    acc = down(t_prev, n_stages - 1, acc)
    out_ref[...] = x1_scr[0:tm] + _rms(acc, m[5:6] * ng[3:4])


def _ffn_kernel(*refs, n_ctx, tm_ctx, t_ctx, tm_lat, t_lat, fuse_oproj):
    per_group = 6 if fuse_oproj else 3
    ctx_in, lat_in = refs[:per_group], refs[per_group:2 * per_group]
    refs = refs[2 * per_group:]
    if not fuse_oproj:
        refs = (None,) + refs
    *weights, outc_ref, outl_ref, hn_scr, x1_scr = refs
    shared = (*weights, hn_scr, x1_scr)
    i = pl.program_id(0)

    def run(group_in, out_ref, tile, tm, seq_len):
        o_refs = group_in[3:] if fuse_oproj else None
        _ffn_tile(group_in[:3], o_refs, out_ref, tile, shared, tm=tm, seq_len=seq_len)

    pl.when(i < n_ctx)(
        functools.partial(run, ctx_in, outc_ref, i, tm_ctx, t_ctx))
    pl.when(i >= n_ctx)(
        functools.partial(run, lat_in, outl_ref, i - n_ctx, tm_lat, t_lat))


def _ffn(xc, xl, mod, ng, w_up, conv_w, conv_b, w_down, *, layer, t_ctx, t_lat, ctx_row,
         oc=None, ol=None, w_o=None):
    d = xc.shape[1]
    fuse = oc is not None
    tm_ctx, tm_lat = min(TOKEN_TILE, t_ctx), min(TOKEN_TILE, t_lat)
    g = _TwoGroups(xc.shape[0] // tm_ctx, xl.shape[0] // tm_lat, tm_lat, t_lat, ctx_row)

    def group_specs(n_rows, tm, tile):
        hb = tm // HALO
        last = n_rows // HALO - 1
        return [pl.BlockSpec((tm, d), lambda i: (tile(i), 0)),
                pl.BlockSpec((HALO, d), lambda i: (jnp.maximum(tile(i) * hb - 1, 0), 0)),
                pl.BlockSpec((HALO, d), lambda i: (jnp.minimum((tile(i) + 1) * hb, last), 0))]

    args, specs = [], []
    for x, o, tm, tile in ((xc, oc, tm_ctx, g.ctx), (xl, ol, tm_lat, g.lat)):
        for a in ([x, o] if fuse else [x]):
            args += [a, a, a]
            specs += group_specs(x.shape[0], tm, tile)
    consts = ([w_o] if fuse else []) + [mod, ng, w_up, conv_w, conv_b, w_down]
    args += consts
    for a in consts:
        if a is mod:
            specs.append(g.mod_spec(d))
        else:
            specs.append(_const_spec(a.shape, layer if a is w_up or a is w_down else None))
    return pl.pallas_call(
        functools.partial(_ffn_kernel, n_ctx=g.n_ctx, tm_ctx=tm_ctx, t_ctx=t_ctx,
                          tm_lat=tm_lat, t_lat=t_lat, fuse_oproj=fuse),
        grid=(g.n_ctx + g.n_lat,),
        in_specs=specs,
        out_specs=[pl.BlockSpec((tm_ctx, d), lambda i: (g.ctx(i), 0)),
                   pl.BlockSpec((tm_lat, d), lambda i: (g.lat(i), 0))],
        out_shape=[jax.ShapeDtypeStruct(xc.shape, F32), jax.ShapeDtypeStruct(xl.shape, F32)],
        scratch_shapes=[pltpu.VMEM((max(tm_ctx, tm_lat) + HALO, d), BF16),
                        pltpu.VMEM((max(tm_ctx, tm_lat), d), F32)],
        compiler_params=_params("arbitrary"),
        name="ffn_oproj" if fuse else "ffn",
    )(*args)


def _store_keys(k_ref, kn, krb):
    for h in range(N_HEADS):
        lo = h * HEAD_PAD
        k_ref[:, lo:lo + NOPE_DIM] = kn[:, h * NOPE_DIM:(h + 1) * NOPE_DIM].astype(BF16)
        k_ref[:, lo + NOPE_DIM:lo + HEAD_PAD] = krb


def _store_values_t(vt_ref, wuvt_ref, ckvb):
    for lo in range(0, N_HEADS * V_DIM, VT_DOT_ROWS):
        vt_ref[0, lo:lo + VT_DOT_ROWS, :] = _dot_nt(
            wuvt_ref[lo:lo + VT_DOT_ROWS, :], ckvb).astype(BF16)


def _mla_proj_kernel(*refs, rope):
    (x_ref, mod_ref, ng_ref, wdown_ref, gkv_ref, gq_ref, wuq_ref,
     wuk_ref, wuvt_ref, *refs) = refs
    if rope:
        cos_ref, sin_ref, q_ref, k_ref, vt_ref = refs
        cos = cos_ref[...]
        sin = sin_ref[...]
    else:
        q_ref, k_ref, vt_ref, ckv_ref, kr_ref = refs
    m = mod_ref[0]
    ng = ng_ref[...]
    hb = (_rms(x_ref[...], ng[0:1] * (1.0 + m[1:2])) + m[0:1]).astype(BF16)
    down = _mdot(hb, wdown_ref[...])
    ckv = _rms(down[:, :KV_RANK], gkv_ref[...])
    kr = down[:, KV_RANK:KV_RANK + ROPE_PAD]
    if rope:
        kr = kr * cos + pltpu.roll(kr, ROPE_DIM, axis=1) * sin
    else:
        ckv_ref[...] = ckv
        kr_ref[...] = kr[:, :ROPE_DIM]
    ql = _rms(down[:, KV_RANK + ROPE_PAD:], gq_ref[...]).astype(BF16)
    q = _mdot(ql, wuq_ref[...])
    ckvb = ckv.astype(BF16)
    _store_values_t(vt_ref, wuvt_ref, ckvb)
    _store_keys(k_ref, _mdot(ckvb, wuk_ref[...]), kr.astype(BF16))
    for h in range(N_HEADS):
        lo = h * HEAD_PAD
        mid = lo + NOPE_DIM
        q_ref[:, lo:mid] = (q[:, lo:mid] * QK_SCALE_LOG2E).astype(BF16)
        qr = q[:, mid:lo + HEAD_PAD]
        if rope:
            qr = qr * cos + pltpu.roll(qr, ROPE_DIM, axis=1) * sin
        q_ref[:, mid:lo + HEAD_PAD] = (qr * QK_SCALE_LOG2E).astype(BF16)


def _mla_proj(x, mod, ng, w, *, tm, seq_len, mod_row, rope_tables=None):
    n, d = x.shape
    rope = rope_tables is not None
    tiles_per_seq = seq_len // tm
    consts = [ng, w["down_rope"] if rope else w["down"], w["g_kv"], w["g_q"],
              w["uq_rope"] if rope else w["uq"], w["uk"], w["uv_t"]]
    args = [x, mod] + consts
    specs = [pl.BlockSpec((tm, d), lambda i: (i, 0)),
             pl.BlockSpec((1, 6, d), lambda i: (mod_row(i * tm), 0, 0))]
    specs += [_const_spec(a.shape) for a in consts]
    row = lambda i: (i, 0)
    col = lambda i: (i // tiles_per_seq, 0, i % tiles_per_seq)
    out_shape = [jax.ShapeDtypeStruct((n, N_HEADS * HEAD_PAD), BF16),
                 jax.ShapeDtypeStruct((n, N_HEADS * HEAD_PAD), BF16),
                 jax.ShapeDtypeStruct((n // seq_len, N_HEADS * V_DIM, seq_len), BF16)]
    out_specs = [pl.BlockSpec((tm, N_HEADS * HEAD_PAD), row),
                 pl.BlockSpec((tm, N_HEADS * HEAD_PAD), row),
                 pl.BlockSpec((1, N_HEADS * V_DIM, tm), col)]
    if rope:
        args += list(rope_tables)
        specs += [pl.BlockSpec((tm, ROPE_PAD), lambda i: (i % tiles_per_seq, 0))] * 2
    else:
        out_shape += [jax.ShapeDtypeStruct((n, KV_RANK), F32),
                      jax.ShapeDtypeStruct((n, ROPE_DIM), F32)]
        out_specs += [pl.BlockSpec((tm, KV_RANK), row),
                      pl.BlockSpec((tm, ROPE_DIM), row)]
    return pl.pallas_call(
        functools.partial(_mla_proj_kernel, rope=rope),
        grid=(n // tm,),
        in_specs=specs,
        out_specs=out_specs,
        out_shape=out_shape,
        compiler_params=_params("parallel"),
        name="mla_proj_rope" if rope else "mla_proj",
    )(*args)


def _kv_expand_kernel(ckv_ref, kr_ref, wuk_ref, wuvt_ref, k_ref, vt_ref):
    ckvb = ckv_ref[0].astype(BF16)
    _store_values_t(vt_ref, wuvt_ref, ckvb)
    _store_keys(k_ref.at[0], _mdot(ckvb, wuk_ref[...]), kr_ref[0].astype(BF16))


def _kv_expand(ckv, kr_pad, w_uk, w_uv_t):
    b, s, _ = ckv.shape
    blk = lambda i: (i, 0, 0)
    return pl.pallas_call(
        _kv_expand_kernel,
        grid=(b,),
        in_specs=[pl.BlockSpec((1, s, KV_RANK), blk),
                  pl.BlockSpec((1, s, ROPE_PAD), blk),
                  _const_spec(w_uk.shape), _const_spec(w_uv_t.shape)],
        out_specs=[pl.BlockSpec((1, s, N_HEADS * HEAD_PAD), blk),
                   pl.BlockSpec((1, N_HEADS * V_DIM, s), blk)],
        out_shape=[jax.ShapeDtypeStruct((b, s, N_HEADS * HEAD_PAD), BF16),
                   jax.ShapeDtypeStruct((b, N_HEADS * V_DIM, s), BF16)],
        compiler_params=_params("parallel"),
        name="kv_expand",
    )(ckv, kr_pad, w_uk, w_uv_t)


def _attn_kernel(*refs, n_kv, heads, t):
    q_ref = refs[0]
    kv_refs = refs[1:1 + 2 * n_kv]
    o_ref = refs[1 + 2 * n_kv]
    items = [(h, c) for h in range(heads) for c in range(t // Q_CHUNK)]

    def key_blocks(s):
        n_keys = kv_refs[2 * s].shape[1]
        step = min(KEY_CHUNK, n_keys)
        return [(r, r + step) for r in range(0, n_keys, step)]

    def scores(item):
        h, c = item
        q = q_ref[0, c * Q_CHUNK:(c + 1) * Q_CHUNK, h * HEAD_PAD:(h + 1) * HEAD_PAD]
        return [_dot_nt(kv_refs[2 * s][0, lo:hi, h * HEAD_PAD:(h + 1) * HEAD_PAD], q)
                for s in range(n_kv) for lo, hi in key_blocks(s)]

    def softmax_pv(item, blocks):
        h, c = item
        mx = functools.reduce(
            jnp.maximum, [jnp.max(b, axis=0, keepdims=True) for b in blocks])
        ps = [jnp.exp2(b - mx) for b in blocks]
        den = functools.reduce(
            jnp.add, [jnp.sum(p, axis=0, keepdims=True) for p in ps])
        acc = None
        first = 0
        for s in range(n_kv):
            nb = len(key_blocks(s))
            pt = jnp.concatenate([p.astype(BF16) for p in ps[first:first + nb]], axis=0)
            first += nb
            d = _dot(kv_refs[2 * s + 1][0, h * V_DIM:(h + 1) * V_DIM, :], pt)
            acc = d if acc is None else acc + d
        o = (acc * (1.0 / den)).T
        o_ref[0, c * Q_CHUNK:(c + 1) * Q_CHUNK, h * V_DIM:(h + 1) * V_DIM] = o.astype(BF16)

    nxt = scores(items[0])
    for n, item in enumerate(items):
        cur = nxt
        if n + 1 < len(items):
            nxt = scores(items[n + 1])
        softmax_pv(item, cur)


def _attention(q, kvs, *, heads):
    b, t, _ = q.shape
    args = [q]
    specs = [pl.BlockSpec((1, t, heads * HEAD_PAD), lambda bi, h: (bi, 0, h))]
    for k, vt in kvs:
        s = k.shape[1]
        args += [k, vt]
        specs += [pl.BlockSpec((1, s, heads * HEAD_PAD), lambda bi, h: (bi, 0, h)),
                  pl.BlockSpec((1, heads * V_DIM, s), lambda bi, h: (bi, h, 0))]
    return pl.pallas_call(
        functools.partial(_attn_kernel, n_kv=len(kvs), heads=heads, t=t),
        grid=(b, N_HEADS // heads),
        in_specs=specs,
        out_specs=pl.BlockSpec((1, t, heads * V_DIM), lambda bi, h: (bi, 0, h)),
        out_shape=jax.ShapeDtypeStruct((b, t, N_HEADS * V_DIM), BF16),
        compiler_params=_params("parallel", "parallel"),
        name="attention",
    )(*args)


_ROPE_SWAP = np.concatenate([np.arange(16, 32), np.arange(0, 16),
                             np.arange(48, 64), np.arange(32, 48)])


def _mla_weights(w_dq, g_q, w_uq, w_dkv, g_kv, w_uk, w_uv):
    zpad = jnp.zeros((D_MODEL, ROPE_PAD - ROPE_DIM), F32)
    k_rope = w_dkv[:, KV_RANK:]
    down = jnp.concatenate([w_dkv, zpad, w_dq], axis=1)
    down_rope = jnp.concatenate([w_dkv, k_rope[:, _ROPE_SWAP], w_dq], axis=1)
    uq = w_uq.reshape(Q_RANK, N_HEADS, NOPE_DIM + ROPE_DIM)
    q_rope = uq[:, :, NOPE_DIM:]
    hpad = jnp.zeros((Q_RANK, N_HEADS, ROPE_PAD - ROPE_DIM), F32)
    uq_full = jnp.concatenate([uq, hpad], axis=2).reshape(Q_RANK, N_HEADS * HEAD_PAD)
    uq_rope = jnp.concatenate([uq, q_rope[:, :, _ROPE_SWAP]], axis=2)
    uq_rope = uq_rope.reshape(Q_RANK, N_HEADS * HEAD_PAD)
    return dict(down=down.astype(BF16), down_rope=down_rope.astype(BF16),
                g_kv=g_kv.reshape(1, -1), g_q=g_q.reshape(1, -1),
                uq=uq_full.astype(BF16), uq_rope=uq_rope.astype(BF16),
                uk=w_uk.astype(BF16), uv_t=w_uv.T.astype(BF16))


def _rope_tables(t):
    half = ROPE_DIM // 4
    pos = np.arange(t)
    inv = ROPE_BASE ** (-np.arange(half, dtype=np.float64) / half)
    ang_r = (pos // GRID_W)[:, None] * inv[None, :]
    ang_c = (pos % GRID_W)[:, None] * inv[None, :]
    zeros = np.zeros((t, ROPE_PAD - ROPE_DIM))
    cos = np.concatenate([np.cos(ang_r), np.cos(ang_r), np.cos(ang_c),
                          np.cos(ang_c), zeros], axis=1)
    sin = np.concatenate([-np.sin(ang_r), np.sin(ang_r), -np.sin(ang_c),
                          np.sin(ang_c), zeros], axis=1)
    return jnp.asarray(cos, F32), jnp.asarray(sin, F32)


def kernel(x_prompt, x_sample, cache_ckv, cache_krope, c, c_ctx, ada_w, ada_b, norm_g, gm_w_in, gm_b_in, gm_g_v, gm_w_s, gm_b_s, gm_w_out, mla_w_dq, mla_g_q, mla_w_uq, mla_w_dkv, mla_g_kv, mla_w_uk, mla_w_uv, mla_w_o, ffn_w_up, ffn_conv_w, ffn_conv_b, ffn_w_down):
    nb_ctx, t_ctx, d = x_prompt.shape
    nb_lat, t_lat, _ = x_sample.shape
    ctx_row = nb_lat

    cond = jnp.concatenate(
        [c, c_ctx[None, :], jnp.zeros((N_COND - nb_lat - 1, d), F32)], axis=0)
    mod = _ada(cond, ada_w, ada_b).reshape(ada_w.shape[0], N_COND, 6, d)

    mixer_w = (gm_w_in[0].astype(BF16), gm_b_in[0].reshape(1, -1),
               gm_g_v[0].reshape(1, -1), gm_w_s[0].astype(BF16), gm_b_s[0].T,
               gm_w_out[0].astype(BF16))
    mla_w = _mla_weights(mla_w_dq[0], mla_g_q[0], mla_w_uq[0], mla_w_dkv[0],
                         mla_g_kv[0], mla_w_uk[0], mla_w_uv[0])

    groups = dict(t_ctx=t_ctx, t_lat=t_lat, ctx_row=ctx_row)
    xc = x_prompt.reshape(nb_ctx * t_ctx, d)
    xl = x_sample.reshape(nb_lat * t_lat, d)
    depth, _, up_cols = ffn_w_up.shape
    xc, xl, w_up, w_down, w_o = _mixer(
        xc, xl, mod[0], norm_g[0], *mixer_w, tm=TOKEN_TILE, t_lat=t_lat, ctx_row=ctx_row,
        casts=[ffn_w_up.reshape(depth * d, up_cols), ffn_w_down.reshape(depth * D_FF, d),
               mla_w_o[0]])
    w_up = w_up.reshape(depth, d, up_cols)
    w_down = w_down.reshape(depth, D_FF, d)
    ffn_w = [(w_up, ffn_conv_w[i], ffn_conv_b[i].reshape(1, -1), w_down) for i in range(depth)]
    xc, xl = _ffn(xc, xl, mod[0], norm_g[0], *ffn_w[0], layer=0, **groups)

    def attend(x, nb, t, mod_row, heads, rope_tables, cached):
        proj = _mla_proj(x, mod[1], norm_g[1], mla_w, tm=min(TOKEN_TILE, t), seq_len=t,
                         mod_row=mod_row, rope_tables=rope_tables)
        q, k = (a.reshape(nb, t, -1) for a in proj[:2])
        kvs = [(k, proj[2])] + ([cached] if cached is not None else [])
        return _attention(q, kvs, heads=heads).reshape(nb * t, -1), proj[3:]

    oc, (ckv, krope) = attend(xc, nb_ctx, t_ctx, lambda row: ctx_row, N_HEADS, None, None)
    kr_pad = jnp.pad(cache_krope[:, 0], ((0, 0), (0, 0), (0, ROPE_PAD - ROPE_DIM)))
    cached = _kv_expand(cache_ckv[:, 0], kr_pad, mla_w["uk"], mla_w["uv_t"])
    ol, _ = attend(xl, nb_lat, t_lat, lambda row: row // t_lat, LATENT_HEADS_PER_STEP,
                   _rope_tables(t_lat), cached)
    xc, xl = _ffn(xc, xl, mod[1], norm_g[1], *ffn_w[1], layer=1, **groups,
                  oc=oc, ol=ol, w_o=w_o)

    return (xc.reshape(nb_ctx, t_ctx, d), xl.reshape(nb_lat, t_lat, d),
            ckv.reshape(nb_ctx, 1, t_ctx, KV_RANK),
            krope.reshape(nb_ctx, 1, t_ctx, ROPE_DIM))
```
